```python
import jax, jax.numpy as jnp
from jax import lax
import numpy as np

D_MODEL = 1024
BATCH = 8
SEQ = 2048
DEPTH = 2
DEC_BATCH = 128
DEC_SEQ = 1
PAST_LEN = 16384
PAGE_SIZE = 128

D_MIX = D_MODEL
C_A = D_MIX // 2
C_B = D_MIX - C_A
HEAD_A = 64
H_A = C_A // HEAD_A
LORA_DECAY = 32
LORA_AAA = 32
LORA_GATE = 96
N_SHIFT = 3 * C_A + LORA_DECAY + LORA_AAA + LORA_GATE
N_IN = N_SHIFT + 2 * C_B
SPLIT_A = (C_A, 2 * C_A, 3 * C_A, 3 * C_A + LORA_DECAY, 3 * C_A + LORA_DECAY + LORA_AAA)
CONV_W = 4
LRU_BLOCKS = 8
LRU_BW = C_B // LRU_BLOCKS
LRU_C = 8.0
N_GROUPS = 4
EXP_PER_GROUP = 4
N_EXPERTS = N_GROUPS * EXP_PER_GROUP
TOP_K = 2
D_EXPERT = 256
ALPHA = (2 * DEPTH) ** 0.25
BETA = (8 * DEPTH) ** -0.25
LN_EPS = 1e-5
GN_EPS = 64e-5

kernel_name = "hymba_rwkv7_rglru_hmoe_deepnorm_step"


def _layer_norm(x, eps=LN_EPS):
    xf = x.astype(jnp.float32)
    mu = jnp.mean(xf, -1, keepdims=True)
    var = jnp.mean(jnp.square(xf - mu), -1, keepdims=True)
    return (xf - mu) * lax.rsqrt(var + eps)


def _token_shift(p, p_prev, mu):
    prev = jnp.concatenate([p_prev[:, None, :], p[:, :-1]], axis=1)
    return p + (prev - p) * mu


def _wkv7_scan(r, decay, k, v, a_vec, b_vec, s0):
    def step(s, inp):
        r_t, w_t, k_t, v_t, a_t, b_t = inp
        sa = jnp.einsum('bhij,bhj->bhi', s, a_t)
        s = s * w_t[:, :, None, :] + sa[..., None] * b_t[:, :, None, :] + v_t[..., None] * k_t[:, :, None, :]
        return s, jnp.einsum('bhij,bhj->bhi', s, r_t)
    xs = tuple(jnp.moveaxis(t, 1, 0) for t in (r, decay, k, v, a_vec, b_vec))
    s_T, ys = lax.scan(step, s0, xs)
    return jnp.moveaxis(ys, 0, 1), s_T


def _rglru_scan(a, u, h0):
    def step(h, inp):
        a_t, u_t = inp
        h = a_t * h + u_t
        return h, h
    h_T, hs = lax.scan(step, h0, (jnp.moveaxis(a, 1, 0), jnp.moveaxis(u, 1, 0)))
    return jnp.moveaxis(hs, 0, 1), h_T


def _causal_conv(x, buf, w, b):
    T = x.shape[1]
    xp = jnp.concatenate([buf.astype(x.dtype), x], axis=1)
    y = b + xp[:, 0:T] * w[0]
    for j in range(1, CONV_W):
        y = y + xp[:, j:j + T] * w[j]
    return y, xp[:, T:]


def _time_mix(h, h_prev, s_wkv, conv_buf, lru_h, p):
    B, T, _ = h.shape
    f32 = jnp.float32
    P = h @ p['w_in']
    p_prev = h_prev.astype(h.dtype) @ p['w_in'][:, :N_SHIFT]
    xs = _token_shift(P[..., :N_SHIFT], p_prev, p['mu_shift'])
    r, k, v, wd, ad, gd = jnp.split(xs, SPLIT_A, axis=-1)
    w_log = -jax.nn.softplus(-(p['w0'] + jnp.tanh(wd) @ p['w_decay_up'])) - 0.5
    decay = jnp.exp(-jnp.exp(w_log.astype(f32)))
    a = jax.nn.sigmoid(p['a0'] + ad @ p['w_a_up'])
    g = jax.nn.sigmoid(gd) @ p['w_g_up']

    def heads(t):
        return t.reshape(B, T, H_A, HEAD_A).astype(f32)
    kk = heads(k * p['k_k'])
    kk = kk / jnp.maximum(jnp.sqrt(jnp.sum(kk * kk, -1, keepdims=True)), 1e-12)
    k = k * (1 + (a - 1) * p['k_a'])
    r_h, k_h, v_h, a_h = heads(r), heads(k), heads(v), heads(a)
    y_a, s_new = _wkv7_scan(r_h, heads(decay), k_h, v_h, -kk, kk * a_h, s_wkv.astype(f32))
    mu = jnp.mean(y_a, -1, keepdims=True)
    var = jnp.mean(jnp.square(y_a - mu), -1, keepdims=True)
    y_a = ((y_a - mu) * lax.rsqrt(var + GN_EPS)).reshape(B, T, C_A) * p['lnx_gain'] + p['lnx_bias']
    bonus = jnp.sum(r_h * k_h * p['r_k'], -1, keepdims=True) * v_h
    y_a = (y_a + bonus.reshape(B, T, C_A)) * g
    gate_b = P[..., N_SHIFT:N_SHIFT + C_B]
    rec_b = P[..., N_SHIFT + C_B:]
    xc, conv_new = _causal_conv(rec_b, conv_buf, p['conv_w'], p['conv_b'])
    xcb = xc.reshape(B, T, LRU_BLOCKS, LRU_BW)
    r_t = jax.nn.sigmoid(jnp.einsum('btnc,ncd->btnd', xcb, p['w_rgate']).reshape(B, T, C_B) + p['b_rgate'])
    i_t = jax.nn.sigmoid(jnp.einsum('btnc,ncd->btnd', xcb, p['w_igate']).reshape(B, T, C_B) + p['b_igate'])
    log_a = -LRU_C * r_t.astype(f32) * jax.nn.softplus(-p['lru_lambda'].astype(f32))
    mult = jnp.sqrt(-jnp.expm1(2.0 * log_a))
    u = mult * (i_t * xc).astype(f32)
    hs, lru_new = _rglru_scan(jnp.exp(log_a), u, lru_h.astype(f32))
    y_b = hs * jax.nn.gelu(gate_b.astype(f32))
    y = jnp.concatenate([y_a, y_b], axis=-1).astype(h.dtype) @ p['w_out']
    return y, s_new, h[:, -1], conv_new, lru_new


def _hier_moe(h, p):
    B, T, D = h.shape
    f32 = jnp.float32
    t = h.reshape(B * T, D)
    gl = (t @ p['w_router_group'] + p['b_router_group']).astype(f32)
    gp = jax.nn.softmax(gl, -1)
    gi = jnp.argmax(gl, -1)
    p_grp = jnp.take_along_axis(gp, gi[:, None], -1)
    el = (t @ p['w_router_expert'] + p['b_router_expert']).astype(f32).reshape(-1, N_GROUPS, EXP_PER_GROUP)
    el = jnp.take_along_axis(el, gi[:, None, None], 1)[:, 0]
    top_v, top_i = lax.top_k(jax.nn.softmax(el, -1), TOP_K)
    top_v = top_v / jnp.sum(top_v, -1, keepdims=True) * p_grp
    eid = gi[:, None] * EXP_PER_GROUP + top_i
    comb = jnp.sum(jax.nn.one_hot(eid, N_EXPERTS, dtype=f32) * top_v[..., None], 1)
    hid = jax.nn.silu(jnp.einsum('nd,edf->nef', t, p['w_exp_gate'])) * jnp.einsum('nd,edf->nef', t, p['w_exp_up'])
    y = jnp.einsum('nef,efd->nd', hid * comb[..., None].astype(hid.dtype), p['w_exp_down'])
    return y.reshape(B, T, D)


def _post_norm(z, gain, bias, dtype):
    return (_layer_norm(z) * gain + bias).astype(dtype)


def _layer(x, c, s_wkv, s_shift, s_conv, s_lru, p):
    mod = jax.nn.silu(c) @ p['w_ada'] + p['b_ada']
    sh1, sc1, g1, sh2, sc2, g2 = jnp.split(mod[:, None, :], 6, axis=-1)
    h = (_layer_norm(x) * (1 + sc1) + sh1).astype(x.dtype)
    y, s_wkv, s_shift, s_conv, s_lru = _time_mix(h, s_shift, s_wkv, s_conv, s_lru, p)
    x = _post_norm(ALPHA * x + (1 + g1) * y, p['ln1_gain'], p['ln1_bias'], x.dtype)
    h = (_layer_norm(x) * (1 + sc2) + sh2).astype(x.dtype)
    x = _post_norm(ALPHA * x + (1 + g2) * _hier_moe(h, p), p['ln2_gain'], p['ln2_bias'], x.dtype)
    return x, s_wkv, s_shift, s_conv, s_lru


def _trunk(x, c, st_wkv, st_shift, st_conv, st_lru, layers):
    wkv, shf, cnv, lru = [], [], [], []
    for l in range(DEPTH):
        x, a, b, d, e = _layer(x, c, st_wkv[l], st_shift[l], st_conv[l], st_lru[l], layers[l])
        wkv.append(a); shf.append(b); cnv.append(d); lru.append(e)
    return x, jnp.stack(wkv), jnp.stack(shf), jnp.stack(cnv), jnp.stack(lru)


def setup_inputs(seed: int = 0) -> dict:
    key = jax.random.key(seed)
    ks = iter(jax.random.split(key, 48))
    f32 = jnp.float32

    def nrm(shape, s):
        return jax.random.normal(next(ks), shape, f32) * s

    Dm = D_MODEL
    u = jax.random.uniform(next(ks), (DEPTH, C_B), f32, 0.9, 0.999)
    base = u ** (1.0 / LRU_C)
    lam = jnp.log(base) - jnp.log1p(-base)
    return {
        'x_prompt': nrm((BATCH, SEQ, Dm), 1.0),
        'x_sample': nrm((DEC_BATCH, DEC_SEQ, Dm), 1.0),
        'c_prompt': nrm((BATCH, Dm), 1.0),
        'c_sample': nrm((DEC_BATCH, Dm), 1.0),
        'state_wkv': nrm((DEPTH, DEC_BATCH, H_A, HEAD_A, HEAD_A), 0.5),
        'state_shift': nrm((DEPTH, DEC_BATCH, Dm), 1.0),
        'state_conv': nrm((DEPTH, DEC_BATCH, CONV_W - 1, C_B), 1.0),
        'state_lru': nrm((DEPTH, DEC_BATCH, C_B), 1.0),
        'w_ada': nrm((DEPTH, Dm, 6 * Dm), 0.1 * Dm ** -0.5),
        'b_ada': nrm((DEPTH, 6 * Dm), 0.01),
        'w_in': nrm((DEPTH, Dm, N_IN), Dm ** -0.5),
        'mu_shift': jax.random.uniform(next(ks), (DEPTH, N_SHIFT), f32),
        'w0': jax.random.uniform(next(ks), (DEPTH, C_A), f32, -5.0, 0.0),
        'w_decay_up': nrm((DEPTH, LORA_DECAY, C_A), 0.1),
        'a0': nrm((DEPTH, C_A), 0.1),
        'w_a_up': nrm((DEPTH, LORA_AAA, C_A), 0.5 * LORA_AAA ** -0.5),
        'w_g_up': nrm((DEPTH, LORA_GATE, C_A), LORA_GATE ** -0.5),
        'k_k': 1.0 + nrm((DEPTH, C_A), 0.1),
        'k_a': 1.0 + nrm((DEPTH, C_A), 0.1),
        'r_k': nrm((DEPTH, H_A, HEAD_A), 0.1),
        'lnx_gain': 1.0 + nrm((DEPTH, C_A), 0.05),
        'lnx_bias': nrm((DEPTH, C_A), 0.01),
        'conv_w': nrm((DEPTH, CONV_W, C_B), CONV_W ** -0.5),
        'conv_b': nrm((DEPTH, C_B), 0.01),
        'w_rgate': nrm((DEPTH, LRU_BLOCKS, LRU_BW, LRU_BW), LRU_BW ** -0.5),
        'b_rgate': nrm((DEPTH, C_B), 0.01),
        'w_igate': nrm((DEPTH, LRU_BLOCKS, LRU_BW, LRU_BW), LRU_BW ** -0.5),
        'b_igate': nrm((DEPTH, C_B), 0.01),
        'lru_lambda': lam,
        'w_out': nrm((DEPTH, D_MIX, Dm), BETA * D_MIX ** -0.5),
        'ln1_gain': 1.0 + nrm((DEPTH, Dm), 0.05),
        'ln1_bias': nrm((DEPTH, Dm), 0.01),
        'w_router_group': nrm((DEPTH, Dm, N_GROUPS), Dm ** -0.5),
        'b_router_group': nrm((DEPTH, N_GROUPS), 0.01),
        'w_router_expert': nrm((DEPTH, Dm, N_EXPERTS), Dm ** -0.5),
        'b_router_expert': nrm((DEPTH, N_EXPERTS), 0.01),
        'w_exp_gate': nrm((DEPTH, N_EXPERTS, Dm, D_EXPERT), Dm ** -0.5),
        'w_exp_up': nrm((DEPTH, N_EXPERTS, Dm, D_EXPERT), Dm ** -0.5),
        'w_exp_down': nrm((DEPTH, N_EXPERTS, D_EXPERT, Dm), BETA * D_EXPERT ** -0.5),
        'ln2_gain': 1.0 + nrm((DEPTH, Dm), 0.05),
        'ln2_bias': nrm((DEPTH, Dm), 0.01),
    }


def reference(x_prompt, x_sample, c_prompt, c_sample, state_wkv, state_shift, state_conv, state_lru,
              w_ada, b_ada, w_in, mu_shift, w0, w_decay_up, a0, w_a_up, w_g_up, k_k, k_a, r_k,
              lnx_gain, lnx_bias, conv_w, conv_b, w_rgate, b_rgate, w_igate, b_igate, lru_lambda,
              w_out, ln1_gain, ln1_bias, w_router_group, b_router_group, w_router_expert,
              b_router_expert, w_exp_gate, w_exp_up, w_exp_down, ln2_gain, ln2_bias):
    layers = [dict(w_ada=w_ada[l], b_ada=b_ada[l], w_in=w_in[l], mu_shift=mu_shift[l], w0=w0[l],
                   w_decay_up=w_decay_up[l], a0=a0[l], w_a_up=w_a_up[l], w_g_up=w_g_up[l],
                   k_k=k_k[l], k_a=k_a[l], r_k=r_k[l], lnx_gain=lnx_gain[l], lnx_bias=lnx_bias[l],
                   conv_w=conv_w[l], conv_b=conv_b[l], w_rgate=w_rgate[l], b_rgate=b_rgate[l],
                   w_igate=w_igate[l], b_igate=b_igate[l], lru_lambda=lru_lambda[l], w_out=w_out[l],
                   ln1_gain=ln1_gain[l], ln1_bias=ln1_bias[l], w_router_group=w_router_group[l],
                   b_router_group=b_router_group[l], w_router_expert=w_router_expert[l],
                   b_router_expert=b_router_expert[l], w_exp_gate=w_exp_gate[l],
                   w_exp_up=w_exp_up[l], w_exp_down=w_exp_down[l], ln2_gain=ln2_gain[l],
                   ln2_bias=ln2_bias[l]) for l in range(DEPTH)]
    bp = x_prompt.shape[0]
    z_wkv = jnp.zeros((DEPTH, bp, H_A, HEAD_A, HEAD_A), jnp.float32)
    z_shift = jnp.zeros((DEPTH, bp, D_MODEL), x_prompt.dtype)
    z_conv = jnp.zeros((DEPTH, bp, CONV_W - 1, C_B), x_prompt.dtype)
    z_lru = jnp.zeros((DEPTH, bp, C_B), jnp.float32)
    y_prompt, wkv_p, shift_p, conv_p, lru_p = _trunk(x_prompt, c_prompt, z_wkv, z_shift, z_conv, z_lru, layers)
    y_sample, wkv_s, shift_s, conv_s, lru_s = _trunk(x_sample, c_sample, state_wkv, state_shift,
                                                     state_conv, state_lru, layers)
    return (y_prompt, y_sample, wkv_p, shift_p, conv_p, lru_p, wkv_s, shift_s, conv_s, lru_s)
```

```python
import functools

import jax
import jax.numpy as jnp
from jax import lax
from jax.experimental import pallas as pl
from jax.experimental.pallas import tpu as pltpu

F32 = jnp.float32
BF16 = jnp.bfloat16

D_MODEL = 1024
DEPTH = 2
C_A = 512
C_B = 512
HEAD = 64
N_HEADS = C_A // HEAD
LORA_DECAY = 32
LORA_AAA = 32
LORA_GATE = 96
LORA_PAD = 128
N_SHIFT = 3 * C_A + LORA_DECAY + LORA_AAA + LORA_GATE
N_SHIFT_PAD = 3 * C_A + 3 * LORA_PAD
N_IN_PAD = N_SHIFT_PAD + 2 * C_B
CONV_W = 4
LRU_BLOCKS = 8
LRU_C = 8.0
N_GROUPS = 4
EXP_PER_GROUP = 4
N_EXPERTS = 16
D_EXPERT = 256
ROUTER_LANES = 128
ALPHA = (2 * DEPTH) ** 0.25
LN_EPS = 1e-5
GN_EPS = 64e-5
J_HALF = HEAD // 2
VMEM_LIMIT = 56 * 1024 * 1024


def _params(n_axes=1):
    return pltpu.CompilerParams(dimension_semantics=("arbitrary",) * n_axes,
                                vmem_limit_bytes=VMEM_LIMIT)


def _full(shape):
    return pl.BlockSpec(shape, lambda *_: (0,) * len(shape))


def _const(shape):
    return pl.BlockSpec(shape, lambda *_: (0,) * len(shape), pipeline_mode=pl.Buffered(1))


def _layer_norm(x):
    mu = jnp.mean(x, axis=-1, keepdims=True)
    xc = x - mu
    var = jnp.mean(xc * xc, axis=-1, keepdims=True)
    return xc * lax.rsqrt(var + LN_EPS)


def _softplus(z):
    return jnp.maximum(z, 0.0) + jnp.log1p(jnp.exp(-jnp.abs(z)))


def _sigmoid(z):
    return 1.0 / (1.0 + jnp.exp(-z))


def _gelu_tanh(x):
    c = 0.7978845608028654
    return x * (0.5 * (1.0 + jnp.tanh(c * (x + 0.044715 * (x * x * x)))))


def _dot(a, b):
    return jnp.dot(a, b, preferred_element_type=F32)


def _split(x, precise=True):
    hi = x.astype(BF16)
    return hi, ((x - hi.astype(F32)).astype(BF16) if precise else None)


def _mm(xs, w_hi, w_lo=None):
    hi, lo = xs
    out = _dot(hi, w_hi)
    if w_lo is not None:
        out = out + _dot(lo, w_hi) + _dot(hi, w_lo)
    return out


def _split_dot(x, w_bf16):
    hi, lo = _split(x)
    return _dot(hi, w_bf16) + _dot(lo, w_bf16)


def _load_time_major(x_ref, tm_ref, batch_major, tt):
    if not batch_major:
        return x_ref[...]
    for t in range(tt):
        tm_ref[t] = x_ref[:, t, :]
    return tm_ref[...]


def _ada_kernel(c_ref, w_ref, b_ref, o_ref):
    c = c_ref[...]
    w_hi, w_lo = _split(w_ref[0])
    o_ref[0] = _mm(_split(c * _sigmoid(c)), w_hi, w_lo) + b_ref[0]


def _ada_call(c_all, w_ada, b_ada):
    n = c_all.shape[0]
    tn = 1536
    return pl.pallas_call(
        _ada_kernel,
        grid=(DEPTH, 6 * D_MODEL // tn),
        in_specs=[pl.BlockSpec((n, D_MODEL), lambda l, j: (0, 0)),
                  pl.BlockSpec((1, D_MODEL, tn), lambda l, j: (l, 0, j)),
                  pl.BlockSpec((1, 1, tn), lambda l, j: (l, 0, j))],
        out_specs=pl.BlockSpec((1, n, tn), lambda l, j: (l, 0, j)),
        out_shape=jax.ShapeDtypeStruct((DEPTH, n, 6 * D_MODEL), F32),
        compiler_params=_params(2),
        name="ada_mod",
    )(c_all, w_ada, b_ada.reshape(DEPTH, 1, 6 * D_MODEL))


N_PRE_MATS = 5


def _mix_pre_kernel(precise, batch_major, tt, bsz,
                    x_ref, sh_ref, sc_ref, hprev_ref, cst_ref, lst_ref, *refs):
    n_mat = N_PRE_MATS * (2 if precise else 1)
    mats = refs[:n_mat]
    if precise:
        (win_ref, win_lo), (wdu_ref, wdu_lo), (wau_ref, wau_lo), (wgu_ref, wgu_lo), (wg_ref, wg_lo) = (
            (mats[2 * i], mats[2 * i + 1]) for i in range(N_PRE_MATS))
    else:
        (win_ref, win_lo), (wdu_ref, wdu_lo), (wau_ref, wau_lo), (wgu_ref, wgu_lo), (wg_ref, wg_lo) = (
            (m, None) for m in mats)
    (mu_ref, w0_ref, a0_ref, kk_ref, ka_ref, rk_ref, cw_ref, cb_ref, bg_ref, lam_ref, gones_ref,
     r_o, w_o, k_o, v_o, a_o, b_o, g_o, bonus_o, yb_o, shift_o, conv_o, lru_o,
     prevp_s, conv_s, lru_s, a_s, u_s, hs_s, *tm_s) = refs[n_mat:]
    rows = tt * bsz

    def win(lo_hi_ref, c0, c1):
        return None if lo_hi_ref is None else lo_hi_ref[:, c0:c1]

    @pl.when(pl.program_id(0) == 0)
    def _():
        prevp_s[...] = _mm(_split(hprev_ref[...], precise), win_ref[:, :N_SHIFT_PAD], win(win_lo, 0, N_SHIFT_PAD))
        conv_s[...] = cst_ref[...]
        lru_s[...] = lst_ref[...]

    x = _load_time_major(x_ref, tm_s[0] if batch_major else None, batch_major, tt)
    h = _layer_norm(x) * (1.0 + sc_ref[...]) + sh_ref[...]
    shift_o[...] = h[tt - 1]
    hb = _split(h.reshape(rows, D_MODEL), precise)

    def shifted(off, width):
        p = _mm(hb, win_ref[:, off:off + width], win(win_lo, off, off + width)).reshape(tt, bsz, width)
        first = prevp_s[:, off:off + width][None]
        prev = jnp.concatenate([first, p[:tt - 1]], axis=0) if tt > 1 else first
        prevp_s[:, off:off + width] = p[tt - 1]
        return p + (prev - p) * mu_ref[:, off:off + width]

    def up(z, w_ref_, w_lo_):
        zs = _split(z.reshape(rows, LORA_PAD), precise)
        return _mm(zs, w_ref_[...], None if w_lo_ is None else w_lo_[...]).reshape(tt, bsz, C_A)

    def gsum(z):
        return _split_dot(z.reshape(rows, C_A), gones_ref[...]).reshape(tt, bsz, C_A)

    r = shifted(0, C_A)
    k = shifted(C_A, C_A)
    v = shifted(2 * C_A, C_A)
    lora = shifted(3 * C_A, 3 * LORA_PAD)
    wd = lora[:, :, 0:LORA_PAD]
    ad = lora[:, :, LORA_PAD:2 * LORA_PAD]
    gd = lora[:, :, 2 * LORA_PAD:3 * LORA_PAD]

    w_log = -_softplus(-(w0_ref[...] + up(jnp.tanh(wd), wdu_ref, wdu_lo))) - 0.5
    decay = jnp.exp(-jnp.exp(w_log))
    a_sig = _sigmoid(a0_ref[...] + up(ad, wau_ref, wau_lo))
    g_o[...] = up(_sigmoid(gd), wgu_ref, wgu_lo)

    kk = k * kk_ref[...]
    kk = kk / jnp.maximum(jnp.sqrt(gsum(kk * kk)), 1e-12)
    k2 = k * (1.0 + (a_sig - 1.0) * ka_ref[...])
    r_o[...] = r
    w_o[...] = decay
    k_o[...] = k2
    v_o[...] = v
    a_o[...] = -kk
    b_o[...] = kk * a_sig
    bonus_o[...] = gsum(r * k2 * rk_ref[...]) * v

    g0, g1 = N_SHIFT_PAD, N_SHIFT_PAD + C_B
    gate_b = _mm(hb, win_ref[:, g0:g1], win(win_lo, g0, g1)).reshape(tt, bsz, C_B)
    rec_b = _mm(hb, win_ref[:, g1:N_IN_PAD], win(win_lo, g1, N_IN_PAD)).reshape(tt, bsz, C_B)
    xp = jnp.concatenate([conv_s[...], rec_b], axis=0)
    xc = cb_ref[...] + xp[0:tt] * cw_ref[0:1, :]
    for j in range(1, CONV_W):
        xc = xc + xp[j:j + tt] * cw_ref[j:j + 1, :]
    conv_new = xp[tt:tt + CONV_W - 1]
    conv_s[...] = conv_new
    conv_o[...] = conv_new

    gates = _mm(_split(xc.reshape(rows, C_B), precise), wg_ref[...],
                None if wg_lo is None else wg_lo[...]) + bg_ref[...]
    r_t = _sigmoid(gates[:, :C_B]).reshape(tt, bsz, C_B)
    i_t = _sigmoid(gates[:, C_B:]).reshape(tt, bsz, C_B)
    log_a = -LRU_C * r_t * _softplus(-lam_ref[...])
    a_s[...] = jnp.exp(log_a)
    u_s[...] = jnp.sqrt(1.0 - jnp.exp(2.0 * log_a)) * (i_t * xc)

    def scan_step(t, hcur):
        hcur = a_s[t] * hcur + u_s[t]
        hs_s[t] = hcur
        return hcur

    h_fin = lax.fori_loop(0, tt, scan_step, lru_s[...])
    lru_s[...] = h_fin
    lru_o[...] = h_fin
    yb_o[...] = hs_s[...] * _gelu_tanh(gate_b)


def _mix_pre_call(x, batch_major, tt, sh1, sc1, h_prev, conv_state, lru_state, wl):
    precise = wl['precise']
    if batch_major:
        bsz, t_len, _ = x.shape
        x_spec = pl.BlockSpec((bsz, tt, D_MODEL), lambda c: (0, c, 0))
    else:
        t_len, bsz, _ = x.shape
        x_spec = pl.BlockSpec((tt, bsz, D_MODEL), lambda c: (c, 0, 0))
    seq = lambda ch: pl.BlockSpec((tt, bsz, ch), lambda c: (c, 0, 0))
    weights = []
    for name in ('w_in', 'w_decay_up', 'w_a_up', 'w_g_up', 'w_gates'):
        weights.append(wl[name])
        if precise:
            weights.append(wl[name + '_lo'])
    weights += [wl['mu'], wl['w0'], wl['a0'], wl['k_k'], wl['k_a'], wl['r_k'], wl['conv_w'], wl['conv_b'],
                wl['b_gates'], wl['lam'], wl['gones']]
    seq_out = jax.ShapeDtypeStruct((t_len, bsz, C_A), F32)
    scratch = [pltpu.VMEM((bsz, N_SHIFT_PAD), F32), pltpu.VMEM((CONV_W - 1, bsz, C_B), F32),
               pltpu.VMEM((bsz, C_B), F32), pltpu.VMEM((tt, bsz, C_B), F32),
               pltpu.VMEM((tt, bsz, C_B), F32), pltpu.VMEM((tt, bsz, C_B), F32)]
    if batch_major:
        scratch.append(pltpu.VMEM((tt, bsz, D_MODEL), F32))
    return pl.pallas_call(
        functools.partial(_mix_pre_kernel, precise, batch_major, tt, bsz),
        grid=(t_len // tt,),
        in_specs=[x_spec, _full((bsz, D_MODEL)), _full((bsz, D_MODEL)), _full((bsz, D_MODEL)),
                  _full((CONV_W - 1, bsz, C_B)), _full((bsz, C_B))] + [_const(w.shape) for w in weights],
        out_specs=[seq(C_A)] * 9 + [_full((bsz, D_MODEL)), _full((CONV_W - 1, bsz, C_B)), _full((bsz, C_B))],
        out_shape=[seq_out] * 9 + [jax.ShapeDtypeStruct((bsz, D_MODEL), F32),
                                   jax.ShapeDtypeStruct((CONV_W - 1, bsz, C_B), F32),
                                   jax.ShapeDtypeStruct((bsz, C_B), F32)],
        scratch_shapes=scratch,
        compiler_params=_params(),
        name="mix_pre",
    )(x, sh1, sc1, h_prev, conv_state, lru_state, *weights)


def _wkv_scan_kernel(tt, a_ref, w_ref, b_ref, k_ref, r_ref, v_ref, s0_ref, y_ref, sfin_ref, s_s):
    @pl.when(pl.program_id(0) == 0)
    def _():
        s_s[...] = s0_ref[...]

    def fold(z):
        return z + pltpu.roll(z, HEAD, axis=1)

    def step(t, carry):
        v = v_ref[t]
        sa = s_s[0] * a_ref[t, 0:1, :]
        for j in range(1, J_HALF):
            sa = sa + s_s[j] * a_ref[t, j:j + 1, :]
        sa = fold(sa)
        y = None
        for j in range(J_HALF):
            s_new = s_s[j] * w_ref[t, j:j + 1, :] + sa * b_ref[t, j:j + 1, :] + v * k_ref[t, j:j + 1, :]
            s_s[j] = s_new
            yj = s_new * r_ref[t, j:j + 1, :]
            y = yj if y is None else y + yj
        y_ref[t] = fold(y)[:, :HEAD]
        return carry

    lax.fori_loop(0, tt, step, 0)

    @pl.when(pl.program_id(0) == pl.num_programs(0) - 1)
    def _():
        sfin_ref[...] = s_s[...]


def _wkv_scan_call(a, w, b, k, r, v, s0, tt):
    t_len = a.shape[0]
    op = pl.BlockSpec((tt, J_HALF, 2 * HEAD), lambda c: (c, 0, 0))
    return pl.pallas_call(
        functools.partial(_wkv_scan_kernel, tt),
        grid=(t_len // tt,),
        in_specs=[op] * 5 + [pl.BlockSpec((tt, HEAD, 2 * HEAD), lambda c: (c, 0, 0)),
                             _full((J_HALF, HEAD, 2 * HEAD))],
        out_specs=[pl.BlockSpec((tt, HEAD, HEAD), lambda c: (c, 0, 0)), _full((J_HALF, HEAD, 2 * HEAD))],
        out_shape=[jax.ShapeDtypeStruct((t_len, HEAD, HEAD), F32),
                   jax.ShapeDtypeStruct((J_HALF, HEAD, 2 * HEAD), F32)],
        scratch_shapes=[pltpu.VMEM((J_HALF, HEAD, 2 * HEAD), F32)],
        compiler_params=_params(),
        name="wkv_scan",
    )(a, w, b, k, r, v, s0)


def _wkv_step_kernel(a_ref, w_ref, b_ref, k_ref, r_ref, v_ref, s_ref, y_ref, so_ref):
    s = s_ref[...]
    row = lax.broadcasted_iota(jnp.int32, (HEAD, HEAD), 0)
    col = lax.broadcasted_iota(jnp.int32, (HEAD, HEAD), 1)
    eye = (row == col).astype(F32)
    sa = jnp.sum(s * a_ref[...], axis=-1, keepdims=True)
    v_col = jnp.sum(eye * v_ref[...], axis=-1, keepdims=True)
    s_new = s * w_ref[...] + sa * b_ref[...] + v_col * k_ref[...]
    so_ref[...] = s_new
    y_col = jnp.sum(s_new * r_ref[...], axis=-1, keepdims=True)
    y_ref[...] = jnp.sum(eye * y_col, axis=-2, keepdims=True)


def _wkv_step_call(a, w, b, k, r, v, s, bb):
    bsz = s.shape[0]
    op = pl.BlockSpec((bb, N_HEADS, 1, HEAD), lambda c: (c, 0, 0, 0))
    st = pl.BlockSpec((bb, N_HEADS, HEAD, HEAD), lambda c: (c, 0, 0, 0))
    return pl.pallas_call(
        _wkv_step_kernel,
        grid=(bsz // bb,),
        in_specs=[op] * 6 + [st],
        out_specs=[op, st],
        out_shape=[jax.ShapeDtypeStruct((bsz, N_HEADS, 1, HEAD), F32),
                   jax.ShapeDtypeStruct((bsz, N_HEADS, HEAD, HEAD), F32)],
        compiler_params=_params(),
        name="wkv_step",
    )(a, w, b, k, r, v, s)


def _route(logits):
    lane = lax.broadcasted_iota(jnp.int32, logits.shape, 1)
    neg = jnp.float32(-jnp.inf)
    big = jnp.int32(ROUTER_LANES)
    is_grp = (lane >= N_EXPERTS) & (lane < N_EXPERTS + N_GROUPS)
    gl = jnp.where(is_grp, logits, neg)
    gmax = jnp.max(gl, axis=-1, keepdims=True)
    ge = jnp.where(is_grp, jnp.exp(gl - gmax), 0.0)
    gp = ge / jnp.sum(ge, axis=-1, keepdims=True)
    gi = jnp.min(jnp.where(gl == gmax, lane, big), axis=-1, keepdims=True)
    p_grp = jnp.sum(jnp.where(lane == gi, gp, 0.0), axis=-1, keepdims=True)
    gidx = gi - N_EXPERTS
    in_grp = (lane >= gidx * EXP_PER_GROUP) & (lane < (gidx + 1) * EXP_PER_GROUP)
    el = jnp.where(in_grp, logits, neg)
    emax = jnp.max(el, axis=-1, keepdims=True)
    ee = jnp.where(in_grp, jnp.exp(el - emax), 0.0)
    pe = ee / jnp.sum(ee, axis=-1, keepdims=True)
    pe_m = jnp.where(in_grp, pe, -1.0)
    v1 = jnp.max(pe_m, axis=-1, keepdims=True)
    i1 = jnp.min(jnp.where(pe_m == v1, lane, big), axis=-1, keepdims=True)
    pe_m2 = jnp.where(lane == i1, -1.0, pe_m)
    v2 = jnp.max(pe_m2, axis=-1, keepdims=True)
    i2 = jnp.min(jnp.where(pe_m2 == v2, lane, big), axis=-1, keepdims=True)
    tot = v1 + v2
    return jnp.where(lane == i1, v1 / tot * p_grp, 0.0) + jnp.where(lane == i2, v2 / tot * p_grp, 0.0)


def _mix_post_kernel(precise, batch_major, tt, bsz,
                     ya_ref, bonus_ref, g_ref, yb_ref, x_ref, g1_ref, sh2_ref, sc2_ref, wout_ref, *refs):
    wout_lo = refs[0] if precise else None
    (lnxg_ref, lnxb_ref, gones_ref, ln1g_ref, ln1b_ref, wrh_ref, wrl_ref, br_ref,
     x1_o, h2_o, comb_o, *tm_s) = refs[1:] if precise else refs
    rows = tt * bsz

    def wout(lo_hi_ref, r0, r1):
        return None if lo_hi_ref is None else lo_hi_ref[r0:r1, :]

    def gsum(z):
        return _split_dot(z.reshape(rows, C_A), gones_ref[...]).reshape(tt, bsz, C_A)

    ya = ya_ref[...]
    mu = gsum(ya) * (1.0 / HEAD)
    yc = ya - mu
    var = gsum(yc * yc) * (1.0 / HEAD)
    yn = yc * lax.rsqrt(var + GN_EPS) * lnxg_ref[...] + lnxb_ref[...]
    ya = (yn + bonus_ref[...]) * g_ref[...]
    y = (_mm(_split(ya.reshape(rows, C_A), precise), wout_ref[0:C_A, :], wout(wout_lo, 0, C_A))
         + _mm(_split(yb_ref[...].reshape(rows, C_B), precise), wout_ref[C_A:, :],
               wout(wout_lo, C_A, C_A + C_B))).reshape(tt, bsz, D_MODEL)
    x = _load_time_major(x_ref, tm_s[0] if batch_major else None, batch_major, tt)
    x1 = _layer_norm(ALPHA * x + (1.0 + g1_ref[...]) * y) * ln1g_ref[...] + ln1b_ref[...]
    x1_o[...] = x1
    h2 = (_layer_norm(x1) * (1.0 + sc2_ref[...]) + sh2_ref[...]).reshape(rows, D_MODEL)
    hi = h2.astype(BF16)
    lo = (h2 - hi.astype(F32)).astype(BF16)
    h2_o[...] = hi
    logits = _dot(hi, wrh_ref[...]) + _dot(lo, wrh_ref[...]) + _dot(hi, wrl_ref[...]) + br_ref[...]
    comb_o[...] = _route(logits)


def _mix_post_call(ya, bonus, g, yb, x, batch_major, tt, g1, sh2, sc2, wl):
    t_len, bsz, _ = ya.shape
    rows = tt * bsz
    if batch_major:
        x_spec = pl.BlockSpec((bsz, tt, D_MODEL), lambda c: (0, c, 0))
    else:
        x_spec = pl.BlockSpec((tt, bsz, D_MODEL), lambda c: (c, 0, 0))
    seq = pl.BlockSpec((tt, bsz, C_A), lambda c: (c, 0, 0))
    precise = wl['precise']
    weights = [wl['w_out']] + ([wl['w_out_lo']] if precise else []) + [
        wl['lnx_gain'], wl['lnx_bias'], wl['gones'], wl['ln1_gain'], wl['ln1_bias'],
        wl['w_router_hi'], wl['w_router_lo'], wl['b_router']]
    scratch = [pltpu.VMEM((tt, bsz, D_MODEL), F32)] if batch_major else []
    return pl.pallas_call(
        functools.partial(_mix_post_kernel, precise, batch_major, tt, bsz),
        grid=(t_len // tt,),
        in_specs=[seq] * 4 + [x_spec] + [_full((bsz, D_MODEL))] * 3 + [_const(w.shape) for w in weights],
        out_specs=[pl.BlockSpec((tt, bsz, D_MODEL), lambda c: (c, 0, 0)),
                   pl.BlockSpec((rows, D_MODEL), lambda c: (c, 0)),
                   pl.BlockSpec((rows, ROUTER_LANES), lambda c: (c, 0))],
        out_shape=[jax.ShapeDtypeStruct((t_len, bsz, D_MODEL), F32),
                   jax.ShapeDtypeStruct((t_len * bsz, D_MODEL), BF16),
                   jax.ShapeDtypeStruct((t_len * bsz, ROUTER_LANES), F32)],
        scratch_shapes=scratch,
        compiler_params=_params(),
        name="mix_post",
    )(ya, bonus, g, yb, x, g1, sh2, sc2, *weights)


def _moe_kernel(batch_major_out, tt, bsz,
                h2_ref, comb_ref, x1_ref, g2_ref, wg_ref, wu_ref, wd_ref, ln2g_ref, ln2b_ref,
                o_ref, acc_s):
    e = pl.program_id(1)
    rows = tt * bsz

    @pl.when(e == 0)
    def _():
        acc_s[...] = jnp.zeros_like(acc_s)

    h = h2_ref[...]
    lane = lax.broadcasted_iota(jnp.int32, (rows, ROUTER_LANES), 1)
    ce = jnp.sum(jnp.where(lane == e, comb_ref[...], 0.0), axis=-1, keepdims=True)
    gate = _dot(h, wg_ref[0])
    hid = (gate * _sigmoid(gate)) * _dot(h, wu_ref[0])
    acc_s[...] += _dot((hid * ce).astype(BF16), wd_ref[0])

    @pl.when(e == N_EXPERTS - 1)
    def _():
        moe = acc_s[...].reshape(tt, bsz, D_MODEL)
        out = _layer_norm(ALPHA * x1_ref[...] + (1.0 + g2_ref[...]) * moe) * ln2g_ref[...] + ln2b_ref[...]
        if batch_major_out:
            for t in range(tt):
                o_ref[:, t, :] = out[t]
        else:
            o_ref[...] = out


def _moe_call(h2, comb, x1, batch_major_out, tt, g2, wl):
    t_len, bsz, _ = x1.shape
    rows = tt * bsz
    if batch_major_out:
        o_spec = pl.BlockSpec((bsz, tt, D_MODEL), lambda c, e: (0, c, 0))
        o_shape = jax.ShapeDtypeStruct((bsz, t_len, D_MODEL), F32)
    else:
        o_spec = pl.BlockSpec((tt, bsz, D_MODEL), lambda c, e: (c, 0, 0))
        o_shape = jax.ShapeDtypeStruct((t_len, bsz, D_MODEL), F32)
    return pl.pallas_call(
        functools.partial(_moe_kernel, batch_major_out, tt, bsz),
        grid=(t_len // tt, N_EXPERTS),
        in_specs=[pl.BlockSpec((rows, D_MODEL), lambda c, e: (c, 0)),
                  pl.BlockSpec((rows, ROUTER_LANES), lambda c, e: (c, 0)),
                  pl.BlockSpec((tt, bsz, D_MODEL), lambda c, e: (c, 0, 0)),
                  pl.BlockSpec((bsz, D_MODEL), lambda c, e: (0, 0)),
                  pl.BlockSpec((1, D_MODEL, D_EXPERT), lambda c, e: (e, 0, 0)),
                  pl.BlockSpec((1, D_MODEL, D_EXPERT), lambda c, e: (e, 0, 0)),
                  pl.BlockSpec((1, D_EXPERT, D_MODEL), lambda c, e: (e, 0, 0)),
                  pl.BlockSpec((1, D_MODEL), lambda c, e: (0, 0)),
                  pl.BlockSpec((1, D_MODEL), lambda c, e: (0, 0))],
        out_specs=o_spec,
        out_shape=o_shape,
        scratch_shapes=[pltpu.VMEM((rows, D_MODEL), F32)],
        compiler_params=_params(2),
        name="moe",
    )(h2, comb, x1, g2, wl['w_exp_gate'], wl['w_exp_up'], wl['w_exp_down'], wl['ln2_gain'], wl['ln2_bias'])


def _pad_cols(w, width):
    return jnp.pad(w, ((0, 0), (0, width - w.shape[1])))


def _pad_rows(w, height):
    return jnp.pad(w, ((0, height - w.shape[0]), (0, 0)))


def _block_diag(w):
    n, c, d = w.shape
    eye = jnp.eye(n, dtype=w.dtype)
    return (eye[:, None, :, None] * w[:, :, None, :]).reshape(n * c, n * d)


def _prep_layer(p, l, precise):
    row = lambda v: v[l].reshape(1, -1)
    w_in = p['w_in'][l]
    o1, o2, o3 = 3 * C_A, 3 * C_A + LORA_DECAY, 3 * C_A + LORA_DECAY + LORA_AAA
    pieces = [(0, o1, o1), (o1, o2, LORA_PAD), (o2, o3, LORA_PAD), (o3, N_SHIFT, LORA_PAD)]
    w_in_p = jnp.concatenate([_pad_cols(w_in[:, a:b], wd) for a, b, wd in pieces] + [w_in[:, N_SHIFT:]], axis=1)
    mu = p['mu_shift'][l].reshape(1, -1)
    mu_p = jnp.concatenate([_pad_cols(mu[:, a:b], wd) for a, b, wd in pieces], axis=1)
    w_router = jnp.concatenate([p['w_router_expert'][l], p['w_router_group'][l]], axis=1)
    w_router = _pad_cols(w_router, ROUTER_LANES)
    w_router_hi = w_router.astype(BF16)
    b_router = _pad_cols(jnp.concatenate([p['b_router_expert'][l], p['b_router_group'][l]]).reshape(1, -1),
                         ROUTER_LANES)
    head_id = jnp.arange(C_A) // HEAD
    mats = dict(
        w_in=w_in_p,
        w_decay_up=_pad_rows(p['w_decay_up'][l], LORA_PAD),
        w_a_up=_pad_rows(p['w_a_up'][l], LORA_PAD),
        w_g_up=_pad_rows(p['w_g_up'][l], LORA_PAD),
        w_gates=jnp.concatenate([_block_diag(p['w_rgate'][l]), _block_diag(p['w_igate'][l])], axis=1),
        w_out=p['w_out'][l])
    split_mats = {}
    for name, w in mats.items():
        hi = w.astype(BF16)
        split_mats[name] = hi
        if precise:
            split_mats[name + '_lo'] = (w - hi.astype(F32)).astype(BF16)
    return dict(
        split_mats, precise=precise, mu=mu_p, w0=row(p['w0']), a0=row(p['a0']),
        k_k=row(p['k_k']), k_a=row(p['k_a']), r_k=row(p['r_k']),
        conv_w=p['conv_w'][l], conv_b=row(p['conv_b']),
        b_gates=jnp.concatenate([p['b_rgate'][l], p['b_igate'][l]]).reshape(1, -1),
        lam=row(p['lru_lambda']),
        gones=(head_id[:, None] == head_id[None, :]).astype(BF16),
        lnx_gain=row(p['lnx_gain']), lnx_bias=row(p['lnx_bias']),
        ln1_gain=row(p['ln1_gain']), ln1_bias=row(p['ln1_bias']),
        w_router_hi=w_router_hi, w_router_lo=(w_router - w_router_hi.astype(F32)).astype(BF16),
        b_router=b_router,
        w_exp_gate=p['w_exp_gate'][l].astype(BF16), w_exp_up=p['w_exp_up'][l].astype(BF16),
        w_exp_down=p['w_exp_down'][l].astype(BF16),
        ln2_gain=row(p['ln2_gain']), ln2_bias=row(p['ln2_bias']),
    )


def _to_scan_keys(z):
    t_len, bsz, _ = z.shape
    return z.reshape(t_len, bsz, N_HEADS, 2, J_HALF).transpose(0, 4, 3, 1, 2).reshape(t_len, J_HALF, 2 * HEAD)


def _to_scan_values(z):
    t_len, bsz, _ = z.shape
    zt = z.reshape(t_len, bsz, N_HEADS, HEAD).transpose(0, 3, 1, 2).reshape(t_len, HEAD, bsz * N_HEADS)
    return jnp.concatenate([zt, zt], axis=-1)


def _trunk(x, batch_major, tt, mods, st_wkv, st_shift, st_conv, st_lru, layers, moe_tt):
    wkv_out, shift_out, conv_out, lru_out = [], [], [], []
    for l in range(DEPTH):
        wl = layers[l]
        sh1, sc1, g1, sh2, sc2, g2 = mods[l]
        bm_in = batch_major and l == 0
        conv_state = jnp.swapaxes(st_conv[l], 0, 1)
        (r, w, k, v, a, b, g, bonus, yb, shift_new, conv_new, lru_new) = _mix_pre_call(
            x, bm_in, tt, sh1, sc1, st_shift[l], conv_state, st_lru[l], wl)
        t_len, bsz, _ = r.shape
        if t_len > 1:
            s0 = st_wkv[l].reshape(bsz, N_HEADS, HEAD, 2, J_HALF).transpose(4, 2, 3, 0, 1)
            s0 = s0.reshape(J_HALF, HEAD, 2 * HEAD)
            y_t, s_fin = _wkv_scan_call(_to_scan_keys(a), _to_scan_keys(w), _to_scan_keys(b), _to_scan_keys(k),
                                        _to_scan_keys(r), _to_scan_values(v), s0, tt)
            ya = y_t.reshape(t_len, HEAD, bsz, N_HEADS).transpose(0, 2, 3, 1).reshape(t_len, bsz, C_A)
            s_new = s_fin.reshape(J_HALF, HEAD, 2, bsz, N_HEADS).transpose(3, 4, 1, 2, 0)
            s_new = s_new.reshape(bsz, N_HEADS, HEAD, HEAD)
        else:
            shp = (bsz, N_HEADS, 1, HEAD)
            y4, s_new = _wkv_step_call(a.reshape(shp), w.reshape(shp), b.reshape(shp), k.reshape(shp),
                                       r.reshape(shp), v.reshape(shp), st_wkv[l], 16)
            ya = y4.reshape(1, bsz, C_A)
        x1, h2, comb = _mix_post_call(ya, bonus, g, yb, x, bm_in, tt, g1, sh2, sc2, wl)
        bm_out = batch_major and l == DEPTH - 1
        x = _moe_call(h2, comb, x1, bm_out, moe_tt, g2, wl)
        wkv_out.append(s_new)
        shift_out.append(shift_new)
        conv_out.append(jnp.swapaxes(conv_new, 0, 1))
        lru_out.append(lru_new)
    return x, jnp.stack(wkv_out), jnp.stack(shift_out), jnp.stack(conv_out), jnp.stack(lru_out)


def kernel(x_prompt, x_sample, c_prompt, c_sample, state_wkv, state_shift, state_conv, state_lru, w_ada, b_ada, w_in, mu_shift, w0, w_decay_up, a0, w_a_up, w_g_up, k_k, k_a, r_k, lnx_gain, lnx_bias, conv_w, conv_b, w_rgate, b_rgate, w_igate, b_igate, lru_lambda, w_out, ln1_gain, ln1_bias, w_router_group, b_router_group, w_router_expert, b_router_expert, w_exp_gate, w_exp_up, w_exp_down, ln2_gain, ln2_bias):
    p = dict(w_in=w_in, mu_shift=mu_shift, w0=w0, w_decay_up=w_decay_up, a0=a0, w_a_up=w_a_up,
             w_g_up=w_g_up, k_k=k_k, k_a=k_a, r_k=r_k.reshape(DEPTH, C_A), lnx_gain=lnx_gain,
             lnx_bias=lnx_bias, conv_w=conv_w, conv_b=conv_b, w_rgate=w_rgate, b_rgate=b_rgate,
             w_igate=w_igate, b_igate=b_igate, lru_lambda=lru_lambda, w_out=w_out, ln1_gain=ln1_gain,
             ln1_bias=ln1_bias, w_router_group=w_router_group, b_router_group=b_router_group,
             w_router_expert=w_router_expert, b_router_expert=b_router_expert, w_exp_gate=w_exp_gate,
             w_exp_up=w_exp_up, w_exp_down=w_exp_down, ln2_gain=ln2_gain, ln2_bias=ln2_bias)
    layers = [_prep_layer(p, l, precise=(l == 0)) for l in range(DEPTH)]
    bp, bs = x_prompt.shape[0], x_sample.shape[0]

    mod = _ada_call(jnp.concatenate([c_sample, c_prompt], axis=0), w_ada, b_ada)
    split = lambda m: [m[:, i * D_MODEL:(i + 1) * D_MODEL] for i in range(6)]
    mods_s = [split(mod[l, :bs]) for l in range(DEPTH)]
    mods_p = [split(mod[l, bs:bs + bp]) for l in range(DEPTH)]

    z_wkv = jnp.zeros((DEPTH, bp, N_HEADS, HEAD, HEAD), F32)
    z_shift = jnp.zeros((DEPTH, bp, D_MODEL), F32)
    z_conv = jnp.zeros((DEPTH, bp, CONV_W - 1, C_B), F32)
    z_lru = jnp.zeros((DEPTH, bp, C_B), F32)
    y_p, wkv_p, shift_p, conv_p, lru_p = _trunk(x_prompt, True, 64, mods_p, z_wkv, z_shift, z_conv, z_lru,
                                                layers, 128)
    xs_tm = x_sample.reshape(1, bs, D_MODEL)
    y_s, wkv_s, shift_s, conv_s, lru_s = _trunk(xs_tm, False, 1, mods_s, state_wkv, state_shift, state_conv,
                                                state_lru, layers, 1)
    return (y_p, y_s.reshape(bs, 1, D_MODEL), wkv_p, shift_p, conv_p, lru_p, wkv_s, shift_s, conv_s, lru_s)
```

```python
import functools

import jax
import jax.numpy as jnp
from jax import lax
from jax.experimental import pallas as pl
from jax.experimental.pallas import tpu as pltpu

F32 = jnp.float32
BF16 = jnp.bfloat16

D_MODEL = 1024
DEPTH = 2
C_A = 512
C_B = 512
HEAD = 64
N_HEADS = C_A // HEAD
LORA_DECAY = 32
LORA_AAA = 32
LORA_GATE = 96
LORA_PAD = 128
N_SHIFT = 3 * C_A + LORA_DECAY + LORA_AAA + LORA_GATE
N_SHIFT_PAD = 3 * C_A + 3 * LORA_PAD
N_IN_PAD = N_SHIFT_PAD + 2 * C_B
CONV_W = 4
LRU_BLOCKS = 8
LRU_C = 8.0
N_GROUPS = 4
EXP_PER_GROUP = 4
N_EXPERTS = 16
D_EXPERT = 256
ROUTER_LANES = 128
ALPHA = (2 * DEPTH) ** 0.25
LN_EPS = 1e-5
GN_EPS = 64e-5
J_HALF = HEAD // 2
VMEM_LIMIT = 56 * 1024 * 1024


def _params(n_axes=1):
    return pltpu.CompilerParams(dimension_semantics=("arbitrary",) * n_axes,
                                vmem_limit_bytes=VMEM_LIMIT)


def _full(shape):
    return pl.BlockSpec(shape, lambda *_: (0,) * len(shape))


def _const(shape):
    return pl.BlockSpec(shape, lambda *_: (0,) * len(shape), pipeline_mode=pl.Buffered(1))


def _layer_norm(x):
    mu = jnp.mean(x, axis=-1, keepdims=True)
    xc = x - mu
    var = jnp.mean(xc * xc, axis=-1, keepdims=True)
    return xc * lax.rsqrt(var + LN_EPS)


def _softplus(z):
    return jnp.maximum(z, 0.0) + jnp.log1p(jnp.exp(-jnp.abs(z)))


def _sigmoid(z):
    return 1.0 / (1.0 + jnp.exp(-z))


def _gelu_tanh(x):
    c = 0.7978845608028654
    return x * (0.5 * (1.0 + jnp.tanh(c * (x + 0.044715 * (x * x * x)))))


def _dot(a, b):
    return jnp.dot(a, b, preferred_element_type=F32)


def _split(x, precise=True):
    hi = x.astype(BF16)
    return hi, ((x - hi.astype(F32)).astype(BF16) if precise else None)


def _mm(xs, w_hi, w_lo=None):
    hi, lo = xs
    out = _dot(hi, w_hi)
    if w_lo is not None:
        out = out + _dot(lo, w_hi) + _dot(hi, w_lo)
    return out


def _split_dot(x, w_bf16):
    hi, lo = _split(x)
    return _dot(hi, w_bf16) + _dot(lo, w_bf16)


def _load_time_major(x_ref, tm_ref, batch_major, tt):
    if not batch_major:
        return x_ref[...]
    for t in range(tt):
        tm_ref[t] = x_ref[:, t, :]
    return tm_ref[...]


def _ada_kernel(c_ref, w_ref, b_ref, o_ref):
    c = c_ref[...]
    w_hi, w_lo = _split(w_ref[0])
    o_ref[0] = _mm(_split(c * _sigmoid(c)), w_hi, w_lo) + b_ref[0]


def _ada_call(c_all, w_ada, b_ada):
    n = c_all.shape[0]
    tn = 1536
    return pl.pallas_call(
        _ada_kernel,
        grid=(DEPTH, 6 * D_MODEL // tn),
        in_specs=[pl.BlockSpec((n, D_MODEL), lambda l, j: (0, 0)),
                  pl.BlockSpec((1, D_MODEL, tn), lambda l, j: (l, 0, j)),
                  pl.BlockSpec((1, 1, tn), lambda l, j: (l, 0, j))],
        out_specs=pl.BlockSpec((1, n, tn), lambda l, j: (l, 0, j)),
        out_shape=jax.ShapeDtypeStruct((DEPTH, n, 6 * D_MODEL), F32),
        compiler_params=_params(2),
        name="ada_mod",
    )(c_all, w_ada, b_ada.reshape(DEPTH, 1, 6 * D_MODEL))


N_PRE_MATS = 5


def _mix_pre_kernel(precise, batch_major, tt, bsz,
                    x_ref, sh_ref, sc_ref, hprev_ref, cst_ref, lst_ref, *refs):
    n_mat = N_PRE_MATS * (2 if precise else 1)
    mats = refs[:n_mat]
    if precise:
        (win_ref, win_lo), (wdu_ref, wdu_lo), (wau_ref, wau_lo), (wgu_ref, wgu_lo), (wg_ref, wg_lo) = (
            (mats[2 * i], mats[2 * i + 1]) for i in range(N_PRE_MATS))
    else:
        (win_ref, win_lo), (wdu_ref, wdu_lo), (wau_ref, wau_lo), (wgu_ref, wgu_lo), (wg_ref, wg_lo) = (
            (m, None) for m in mats)
    (mu_ref, w0_ref, a0_ref, kk_ref, ka_ref, rk_ref, cw_ref, cb_ref, bg_ref, lam_ref, gones_ref,
     r_o, w_o, k_o, v_o, a_o, b_o, g_o, bonus_o, yb_o, shift_o, conv_o, lru_o,
     prevp_s, conv_s, lru_s, a_s, u_s, hs_s, *tm_s) = refs[n_mat:]
    rows = tt * bsz

    def win(lo_hi_ref, c0, c1):
        return None if lo_hi_ref is None else lo_hi_ref[:, c0:c1]

    @pl.when(pl.program_id(0) == 0)
    def _():
        prevp_s[...] = _mm(_split(hprev_ref[...], precise), win_ref[:, :N_SHIFT_PAD], win(win_lo, 0, N_SHIFT_PAD))
        conv_s[...] = cst_ref[...]
        lru_s[...] = lst_ref[...]

    x = _load_time_major(x_ref, tm_s[0] if batch_major else None, batch_major, tt)
    h = _layer_norm(x) * (1.0 + sc_ref[...]) + sh_ref[...]
    shift_o[...] = h[tt - 1]
    hb = _split(h.reshape(rows, D_MODEL), precise)

    def shifted(off, width):
        p = _mm(hb, win_ref[:, off:off + width], win(win_lo, off, off + width)).reshape(tt, bsz, width)
        first = prevp_s[:, off:off + width][None]
        prev = jnp.concatenate([first, p[:tt - 1]], axis=0) if tt > 1 else first
        prevp_s[:, off:off + width] = p[tt - 1]
        return p + (prev - p) * mu_ref[:, off:off + width]

    def up(z, w_ref_, w_lo_):
        zs = _split(z.reshape(rows, LORA_PAD), precise)
        return _mm(zs, w_ref_[...], None if w_lo_ is None else w_lo_[...]).reshape(tt, bsz, C_A)

    def gsum(z):
        return _split_dot(z.reshape(rows, C_A), gones_ref[...]).reshape(tt, bsz, C_A)

    r = shifted(0, C_A)
    k = shifted(C_A, C_A)
    v = shifted(2 * C_A, C_A)
    lora = shifted(3 * C_A, 3 * LORA_PAD)
    wd = lora[:, :, 0:LORA_PAD]
    ad = lora[:, :, LORA_PAD:2 * LORA_PAD]
    gd = lora[:, :, 2 * LORA_PAD:3 * LORA_PAD]

    w_log = -_softplus(-(w0_ref[...] + up(jnp.tanh(wd), wdu_ref, wdu_lo))) - 0.5
    decay = jnp.exp(-jnp.exp(w_log))
    a_sig = _sigmoid(a0_ref[...] + up(ad, wau_ref, wau_lo))
    g_o[...] = up(_sigmoid(gd), wgu_ref, wgu_lo)

    kk = k * kk_ref[...]
    kk = kk / jnp.maximum(jnp.sqrt(gsum(kk * kk)), 1e-12)
    k2 = k * (1.0 + (a_sig - 1.0) * ka_ref[...])
    r_o[...] = r
    w_o[...] = decay
    k_o[...] = k2
    v_o[...] = v
    a_o[...] = -kk
    b_o[...] = kk * a_sig
    bonus_o[...] = gsum(r * k2 * rk_ref[...]) * v

    g0, g1 = N_SHIFT_PAD, N_SHIFT_PAD + C_B
    gate_b = _mm(hb, win_ref[:, g0:g1], win(win_lo, g0, g1)).reshape(tt, bsz, C_B)
    rec_b = _mm(hb, win_ref[:, g1:N_IN_PAD], win(win_lo, g1, N_IN_PAD)).reshape(tt, bsz, C_B)
    xp = jnp.concatenate([conv_s[...], rec_b], axis=0)
    xc = cb_ref[...] + xp[0:tt] * cw_ref[0:1, :]
    for j in range(1, CONV_W):
        xc = xc + xp[j:j + tt] * cw_ref[j:j + 1, :]
    conv_new = xp[tt:tt + CONV_W - 1]
    conv_s[...] = conv_new
    conv_o[...] = conv_new

    gates = _mm(_split(xc.reshape(rows, C_B), precise), wg_ref[...],
                None if wg_lo is None else wg_lo[...]) + bg_ref[...]
    r_t = _sigmoid(gates[:, :C_B]).reshape(tt, bsz, C_B)
    i_t = _sigmoid(gates[:, C_B:]).reshape(tt, bsz, C_B)
    log_a = -LRU_C * r_t * _softplus(-lam_ref[...])
    a_s[...] = jnp.exp(log_a)
    u_s[...] = jnp.sqrt(1.0 - jnp.exp(2.0 * log_a)) * (i_t * xc)

    def scan_step(t, hcur):
        hcur = a_s[t] * hcur + u_s[t]
        hs_s[t] = hcur
        return hcur

    h_fin = lax.fori_loop(0, tt, scan_step, lru_s[...])
    lru_s[...] = h_fin
    lru_o[...] = h_fin
    yb_o[...] = hs_s[...] * _gelu_tanh(gate_b)


def _mix_pre_call(x, batch_major, tt, sh1, sc1, h_prev, conv_state, lru_state, wl):
    precise = wl['precise']
    if batch_major:
        bsz, t_len, _ = x.shape
        x_spec = pl.BlockSpec((bsz, tt, D_MODEL), lambda c: (0, c, 0))
    else:
        t_len, bsz, _ = x.shape
        x_spec = pl.BlockSpec((tt, bsz, D_MODEL), lambda c: (c, 0, 0))
    seq = lambda ch: pl.BlockSpec((tt, bsz, ch), lambda c: (c, 0, 0))
    weights = []
    for name in ('w_in', 'w_decay_up', 'w_a_up', 'w_g_up', 'w_gates'):
        weights.append(wl[name])
        if precise:
            weights.append(wl[name + '_lo'])
    weights += [wl['mu'], wl['w0'], wl['a0'], wl['k_k'], wl['k_a'], wl['r_k'], wl['conv_w'], wl['conv_b'],
                wl['b_gates'], wl['lam'], wl['gones']]
    seq_out = jax.ShapeDtypeStruct((t_len, bsz, C_A), F32)
    scratch = [pltpu.VMEM((bsz, N_SHIFT_PAD), F32), pltpu.VMEM((CONV_W - 1, bsz, C_B), F32),
               pltpu.VMEM((bsz, C_B), F32), pltpu.VMEM((tt, bsz, C_B), F32),
               pltpu.VMEM((tt, bsz, C_B), F32), pltpu.VMEM((tt, bsz, C_B), F32)]
    if batch_major:
        scratch.append(pltpu.VMEM((tt, bsz, D_MODEL), F32))
    return pl.pallas_call(
        functools.partial(_mix_pre_kernel, precise, batch_major, tt, bsz),
        grid=(t_len // tt,),
        in_specs=[x_spec, _full((bsz, D_MODEL)), _full((bsz, D_MODEL)), _full((bsz, D_MODEL)),
                  _full((CONV_W - 1, bsz, C_B)), _full((bsz, C_B))] + [_const(w.shape) for w in weights],
        out_specs=[seq(C_A)] * 9 + [_full((bsz, D_MODEL)), _full((CONV_W - 1, bsz, C_B)), _full((bsz, C_B))],
        out_shape=[seq_out] * 9 + [jax.ShapeDtypeStruct((bsz, D_MODEL), F32),
                                   jax.ShapeDtypeStruct((CONV_W - 1, bsz, C_B), F32),
                                   jax.ShapeDtypeStruct((bsz, C_B), F32)],
        scratch_shapes=scratch,
        compiler_params=_params(),
        name="mix_pre",
    )(x, sh1, sc1, h_prev, conv_state, lru_state, *weights)


I_LO = HEAD // 2
SUB = 8
N_STRIP = I_LO // SUB


def _strips(ref, *lead):
    return [ref[(*lead, pl.ds(q * SUB, SUB), slice(None))] for q in range(N_STRIP)]


def _row(ref, *idx):
    return jnp.broadcast_to(ref[(*idx, slice(None))], (SUB, 2 * HEAD))


def _wkv_scan_kernel(tt, an_ref, w_ref, b_ref, k_ref, r_ref, v_ref, a0_ref, s0_ref,
                     y_ref, sfin_ref, s_s, sa_s):
    @pl.when(pl.program_id(0) == 0)
    def _():
        sa = [None] * N_STRIP
        for j in range(HEAD):
            a8 = _row(a0_ref, pl.ds(j, 1))
            for q, s in enumerate(_strips(s0_ref, j)):
                s_s[j, pl.ds(q * SUB, SUB), :] = s
                sa[q] = s * a8 if sa[q] is None else sa[q] + s * a8
        for q in range(N_STRIP):
            sa_s[pl.ds(q * SUB, SUB), :] = sa[q]

    def step(t, sa):
        v = _strips(v_ref, t)
        y = [None] * N_STRIP
        sa_next = [None] * N_STRIP
        for j in range(HEAD):
            w8, b8, k8, r8, a8 = (_row(ref, t, pl.ds(j, 1)) for ref in (w_ref, b_ref, k_ref, r_ref, an_ref))
            for q in range(N_STRIP):
                rows = pl.ds(q * SUB, SUB)
                s_new = s_s[j, rows, :] * w8 + sa[q] * b8 + v[q] * k8
                s_s[j, rows, :] = s_new
                yq, aq = s_new * r8, s_new * a8
                y[q] = yq if y[q] is None else y[q] + yq
                sa_next[q] = aq if sa_next[q] is None else sa_next[q] + aq
        for q in range(N_STRIP):
            y_ref[t, pl.ds(q * SUB, SUB), :] = y[q]
        return tuple(sa_next)

    sa_fin = lax.fori_loop(0, tt, step, tuple(_strips(sa_s)))
    for q in range(N_STRIP):
        sa_s[pl.ds(q * SUB, SUB), :] = sa_fin[q]

    @pl.when(pl.program_id(0) == pl.num_programs(0) - 1)
    def _():
        sfin_ref[...] = s_s[...]


def _wkv_scan_call(a_next, w, b, k, r, v, a0, s0, tt):
    t_len = w.shape[0]
    op = pl.BlockSpec((tt, HEAD, 2 * HEAD), lambda c: (c, 0, 0))
    val = pl.BlockSpec((tt, I_LO, 2 * HEAD), lambda c: (c, 0, 0))
    state = _full((HEAD, I_LO, 2 * HEAD))
    return pl.pallas_call(
        functools.partial(_wkv_scan_kernel, tt),
        grid=(t_len // tt,),
        in_specs=[op] * 5 + [val, _full((HEAD, 2 * HEAD)), state],
        out_specs=[val, state],
        out_shape=[jax.ShapeDtypeStruct((t_len, I_LO, 2 * HEAD), F32),
                   jax.ShapeDtypeStruct((HEAD, I_LO, 2 * HEAD), F32)],
        scratch_shapes=[pltpu.VMEM((HEAD, I_LO, 2 * HEAD), F32), pltpu.VMEM((I_LO, 2 * HEAD), F32)],
        compiler_params=_params(),
        name="wkv_scan",
    )(a_next, w, b, k, r, v, a0, s0)


def _wkv_step_kernel(a_ref, w_ref, b_ref, k_ref, r_ref, v_ref, s_ref, y_ref, so_ref):
    s = s_ref[...]
    row = lax.broadcasted_iota(jnp.int32, (HEAD, HEAD), 0)
    col = lax.broadcasted_iota(jnp.int32, (HEAD, HEAD), 1)
    eye = (row == col).astype(F32)
    sa = jnp.sum(s * a_ref[...], axis=-1, keepdims=True)
    v_col = jnp.sum(eye * v_ref[...], axis=-1, keepdims=True)
    s_new = s * w_ref[...] + sa * b_ref[...] + v_col * k_ref[...]
    so_ref[...] = s_new
    y_col = jnp.sum(s_new * r_ref[...], axis=-1, keepdims=True)
    y_ref[...] = jnp.sum(eye * y_col, axis=-2, keepdims=True)


def _wkv_step_call(a, w, b, k, r, v, s, bb):
    bsz = s.shape[0]
    op = pl.BlockSpec((bb, N_HEADS, 1, HEAD), lambda c: (c, 0, 0, 0))
    st = pl.BlockSpec((bb, N_HEADS, HEAD, HEAD), lambda c: (c, 0, 0, 0))
    return pl.pallas_call(
        _wkv_step_kernel,
        grid=(bsz // bb,),
        in_specs=[op] * 6 + [st],
        out_specs=[op, st],
        out_shape=[jax.ShapeDtypeStruct((bsz, N_HEADS, 1, HEAD), F32),
                   jax.ShapeDtypeStruct((bsz, N_HEADS, HEAD, HEAD), F32)],
        compiler_params=_params(),
        name="wkv_step",
    )(a, w, b, k, r, v, s)


def _route(logits):
    lane = lax.broadcasted_iota(jnp.int32, logits.shape, 1)
    neg = jnp.float32(-jnp.inf)
    big = jnp.int32(ROUTER_LANES)
    is_grp = (lane >= N_EXPERTS) & (lane < N_EXPERTS + N_GROUPS)
    gl = jnp.where(is_grp, logits, neg)
    gmax = jnp.max(gl, axis=-1, keepdims=True)
    ge = jnp.where(is_grp, jnp.exp(gl - gmax), 0.0)
    gp = ge / jnp.sum(ge, axis=-1, keepdims=True)
    gi = jnp.min(jnp.where(gl == gmax, lane, big), axis=-1, keepdims=True)
    p_grp = jnp.sum(jnp.where(lane == gi, gp, 0.0), axis=-1, keepdims=True)
    gidx = gi - N_EXPERTS
    in_grp = (lane >= gidx * EXP_PER_GROUP) & (lane < (gidx + 1) * EXP_PER_GROUP)
    el = jnp.where(in_grp, logits, neg)
    emax = jnp.max(el, axis=-1, keepdims=True)
    ee = jnp.where(in_grp, jnp.exp(el - emax), 0.0)
    pe = ee / jnp.sum(ee, axis=-1, keepdims=True)
    pe_m = jnp.where(in_grp, pe, -1.0)
    v1 = jnp.max(pe_m, axis=-1, keepdims=True)
    i1 = jnp.min(jnp.where(pe_m == v1, lane, big), axis=-1, keepdims=True)
    pe_m2 = jnp.where(lane == i1, -1.0, pe_m)
    v2 = jnp.max(pe_m2, axis=-1, keepdims=True)
    i2 = jnp.min(jnp.where(pe_m2 == v2, lane, big), axis=-1, keepdims=True)
    tot = v1 + v2
    return jnp.where(lane == i1, v1 / tot * p_grp, 0.0) + jnp.where(lane == i2, v2 / tot * p_grp, 0.0)


def _mix_post_kernel(precise, batch_major, tt, bsz,
                     ya_ref, bonus_ref, g_ref, yb_ref, x_ref, g1_ref, sh2_ref, sc2_ref, wout_ref, *refs):
    wout_lo = refs[0] if precise else None
    (lnxg_ref, lnxb_ref, gones_ref, ln1g_ref, ln1b_ref, wrh_ref, wrl_ref, br_ref,
     x1_o, h2_o, comb_o, *tm_s) = refs[1:] if precise else refs
    rows = tt * bsz

    def wout(lo_hi_ref, r0, r1):
        return None if lo_hi_ref is None else lo_hi_ref[r0:r1, :]

    def gsum(z):
        return _split_dot(z.reshape(rows, C_A), gones_ref[...]).reshape(tt, bsz, C_A)

    ya = ya_ref[...]
    mu = gsum(ya) * (1.0 / HEAD)
    yc = ya - mu
    var = gsum(yc * yc) * (1.0 / HEAD)
    yn = yc * lax.rsqrt(var + GN_EPS) * lnxg_ref[...] + lnxb_ref[...]
    ya = (yn + bonus_ref[...]) * g_ref[...]
    y = (_mm(_split(ya.reshape(rows, C_A), precise), wout_ref[0:C_A, :], wout(wout_lo, 0, C_A))
         + _mm(_split(yb_ref[...].reshape(rows, C_B), precise), wout_ref[C_A:, :],
               wout(wout_lo, C_A, C_A + C_B))).reshape(tt, bsz, D_MODEL)
    x = _load_time_major(x_ref, tm_s[0] if batch_major else None, batch_major, tt)
    x1 = _layer_norm(ALPHA * x + (1.0 + g1_ref[...]) * y) * ln1g_ref[...] + ln1b_ref[...]
    x1_o[...] = x1
    h2 = (_layer_norm(x1) * (1.0 + sc2_ref[...]) + sh2_ref[...]).reshape(rows, D_MODEL)
    hi = h2.astype(BF16)
    lo = (h2 - hi.astype(F32)).astype(BF16)
    h2_o[...] = hi
    logits = _dot(hi, wrh_ref[...]) + _dot(lo, wrh_ref[...]) + _dot(hi, wrl_ref[...]) + br_ref[...]
    comb_o[...] = _route(logits)


def _mix_post_call(ya, bonus, g, yb, x, batch_major, tt, g1, sh2, sc2, wl):
    t_len, bsz, _ = ya.shape
    rows = tt * bsz
    if batch_major:
        x_spec = pl.BlockSpec((bsz, tt, D_MODEL), lambda c: (0, c, 0))
    else:
        x_spec = pl.BlockSpec((tt, bsz, D_MODEL), lambda c: (c, 0, 0))
    seq = pl.BlockSpec((tt, bsz, C_A), lambda c: (c, 0, 0))
    precise = wl['precise']
    weights = [wl['w_out']] + ([wl['w_out_lo']] if precise else []) + [
        wl['lnx_gain'], wl['lnx_bias'], wl['gones'], wl['ln1_gain'], wl['ln1_bias'],
        wl['w_router_hi'], wl['w_router_lo'], wl['b_router']]
    scratch = [pltpu.VMEM((tt, bsz, D_MODEL), F32)] if batch_major else []
    return pl.pallas_call(
        functools.partial(_mix_post_kernel, precise, batch_major, tt, bsz),
        grid=(t_len // tt,),
        in_specs=[seq] * 4 + [x_spec] + [_full((bsz, D_MODEL))] * 3 + [_const(w.shape) for w in weights],
        out_specs=[pl.BlockSpec((tt, bsz, D_MODEL), lambda c: (c, 0, 0)),
                   pl.BlockSpec((rows, D_MODEL), lambda c: (c, 0)),
                   pl.BlockSpec((rows, ROUTER_LANES), lambda c: (c, 0))],
        out_shape=[jax.ShapeDtypeStruct((t_len, bsz, D_MODEL), F32),
                   jax.ShapeDtypeStruct((t_len * bsz, D_MODEL), BF16),
                   jax.ShapeDtypeStruct((t_len * bsz, ROUTER_LANES), F32)],
        scratch_shapes=scratch,
        compiler_params=_params(),
        name="mix_post",
    )(ya, bonus, g, yb, x, g1, sh2, sc2, *weights)


def _moe_kernel(batch_major_out, tt, bsz,
                h2_ref, comb_ref, x1_ref, g2_ref, wg_ref, wu_ref, wd_ref, ln2g_ref, ln2b_ref,
                o_ref, acc_s):
    e = pl.program_id(1)
    rows = tt * bsz

    @pl.when(e == 0)
    def _():
        acc_s[...] = jnp.zeros_like(acc_s)

    h = h2_ref[...]
    lane = lax.broadcasted_iota(jnp.int32, (rows, ROUTER_LANES), 1)
    ce = jnp.sum(jnp.where(lane == e, comb_ref[...], 0.0), axis=-1, keepdims=True)
    gate = _dot(h, wg_ref[0])
    hid = (gate * _sigmoid(gate)) * _dot(h, wu_ref[0])
    acc_s[...] += _dot((hid * ce).astype(BF16), wd_ref[0])

    @pl.when(e == N_EXPERTS - 1)
    def _():
        moe = acc_s[...].reshape(tt, bsz, D_MODEL)
        out = _layer_norm(ALPHA * x1_ref[...] + (1.0 + g2_ref[...]) * moe) * ln2g_ref[...] + ln2b_ref[...]
        if batch_major_out:
            for t in range(tt):
                o_ref[:, t, :] = out[t]
        else:
            o_ref[...] = out


def _moe_call(h2, comb, x1, batch_major_out, tt, g2, wl):
    t_len, bsz, _ = x1.shape
    rows = tt * bsz
    if batch_major_out:
        o_spec = pl.BlockSpec((bsz, tt, D_MODEL), lambda c, e: (0, c, 0))
        o_shape = jax.ShapeDtypeStruct((bsz, t_len, D_MODEL), F32)
    else:
        o_spec = pl.BlockSpec((tt, bsz, D_MODEL), lambda c, e: (c, 0, 0))
        o_shape = jax.ShapeDtypeStruct((t_len, bsz, D_MODEL), F32)
    return pl.pallas_call(
        functools.partial(_moe_kernel, batch_major_out, tt, bsz),
        grid=(t_len // tt, N_EXPERTS),
        in_specs=[pl.BlockSpec((rows, D_MODEL), lambda c, e: (c, 0)),
                  pl.BlockSpec((rows, ROUTER_LANES), lambda c, e: (c, 0)),
                  pl.BlockSpec((tt, bsz, D_MODEL), lambda c, e: (c, 0, 0)),
                  pl.BlockSpec((bsz, D_MODEL), lambda c, e: (0, 0)),
                  pl.BlockSpec((1, D_MODEL, D_EXPERT), lambda c, e: (e, 0, 0)),
                  pl.BlockSpec((1, D_MODEL, D_EXPERT), lambda c, e: (e, 0, 0)),
                  pl.BlockSpec((1, D_EXPERT, D_MODEL), lambda c, e: (e, 0, 0)),
                  pl.BlockSpec((1, D_MODEL), lambda c, e: (0, 0)),
                  pl.BlockSpec((1, D_MODEL), lambda c, e: (0, 0))],
        out_specs=o_spec,
        out_shape=o_shape,
        scratch_shapes=[pltpu.VMEM((rows, D_MODEL), F32)],
        compiler_params=_params(2),
        name="moe",
    )(h2, comb, x1, g2, wl['w_exp_gate'], wl['w_exp_up'], wl['w_exp_down'], wl['ln2_gain'], wl['ln2_bias'])


def _pad_cols(w, width):
    return jnp.pad(w, ((0, 0), (0, width - w.shape[1])))


def _pad_rows(w, height):
    return jnp.pad(w, ((0, height - w.shape[0]), (0, 0)))


def _block_diag(w):
    n, c, d = w.shape
    eye = jnp.eye(n, dtype=w.dtype)
    return (eye[:, None, :, None] * w[:, :, None, :]).reshape(n * c, n * d)


def _prep_layer(p, l, precise):
    row = lambda v: v[l].reshape(1, -1)
    w_in = p['w_in'][l]
    o1, o2, o3 = 3 * C_A, 3 * C_A + LORA_DECAY, 3 * C_A + LORA_DECAY + LORA_AAA
    pieces = [(0, o1, o1), (o1, o2, LORA_PAD), (o2, o3, LORA_PAD), (o3, N_SHIFT, LORA_PAD)]
    w_in_p = jnp.concatenate([_pad_cols(w_in[:, a:b], wd) for a, b, wd in pieces] + [w_in[:, N_SHIFT:]], axis=1)
    mu = p['mu_shift'][l].reshape(1, -1)
    mu_p = jnp.concatenate([_pad_cols(mu[:, a:b], wd) for a, b, wd in pieces], axis=1)
    w_router = jnp.concatenate([p['w_router_expert'][l], p['w_router_group'][l]], axis=1)
    w_router = _pad_cols(w_router, ROUTER_LANES)
    w_router_hi = w_router.astype(BF16)
    b_router = _pad_cols(jnp.concatenate([p['b_router_expert'][l], p['b_router_group'][l]]).reshape(1, -1),
                         ROUTER_LANES)
    head_id = jnp.arange(C_A) // HEAD
    mats = dict(
        w_in=w_in_p,
        w_decay_up=_pad_rows(p['w_decay_up'][l], LORA_PAD),
        w_a_up=_pad_rows(p['w_a_up'][l], LORA_PAD),
        w_g_up=_pad_rows(p['w_g_up'][l], LORA_PAD),
        w_gates=jnp.concatenate([_block_diag(p['w_rgate'][l]), _block_diag(p['w_igate'][l])], axis=1),
        w_out=p['w_out'][l])
    split_mats = {}
    for name, w in mats.items():
        hi = w.astype(BF16)
        split_mats[name] = hi
        if precise:
            split_mats[name + '_lo'] = (w - hi.astype(F32)).astype(BF16)
    return dict(
        split_mats, precise=precise, mu=mu_p, w0=row(p['w0']), a0=row(p['a0']),
        k_k=row(p['k_k']), k_a=row(p['k_a']), r_k=row(p['r_k']),
        conv_w=p['conv_w'][l], conv_b=row(p['conv_b']),
        b_gates=jnp.concatenate([p['b_rgate'][l], p['b_igate'][l]]).reshape(1, -1),
        lam=row(p['lru_lambda']),
        gones=(head_id[:, None] == head_id[None, :]).astype(BF16),
        lnx_gain=row(p['lnx_gain']), lnx_bias=row(p['lnx_bias']),
        ln1_gain=row(p['ln1_gain']), ln1_bias=row(p['ln1_bias']),
        w_router_hi=w_router_hi, w_router_lo=(w_router - w_router_hi.astype(F32)).astype(BF16),
        b_router=b_router,
        w_exp_gate=p['w_exp_gate'][l].astype(BF16), w_exp_up=p['w_exp_up'][l].astype(BF16),
        w_exp_down=p['w_exp_down'][l].astype(BF16),
        ln2_gain=row(p['ln2_gain']), ln2_bias=row(p['ln2_bias']),
    )


def _to_scan_values(z):
    t_len, bsz, _ = z.shape
    return z.reshape(t_len, bsz, N_HEADS, 2, I_LO).transpose(0, 4, 3, 1, 2).reshape(t_len, I_LO, 2 * HEAD)


def _to_scan_keys(z):
    t_len, bsz, _ = z.shape
    zt = z.reshape(t_len, bsz, N_HEADS, HEAD).transpose(0, 3, 1, 2).reshape(t_len, HEAD, bsz * N_HEADS)
    return jnp.concatenate([zt, zt], axis=-1)


def _trunk(x, batch_major, tt, mods, st_wkv, st_shift, st_conv, st_lru, layers, moe_tt):
    wkv_out, shift_out, conv_out, lru_out = [], [], [], []
    for l in range(DEPTH):
        wl = layers[l]
        sh1, sc1, g1, sh2, sc2, g2 = mods[l]
        bm_in = batch_major and l == 0
        conv_state = jnp.swapaxes(st_conv[l], 0, 1)
        (r, w, k, v, a, b, g, bonus, yb, shift_new, conv_new, lru_new) = _mix_pre_call(
            x, bm_in, tt, sh1, sc1, st_shift[l], conv_state, st_lru[l], wl)
        t_len, bsz, _ = r.shape
        if t_len > 1:
            s0 = st_wkv[l].reshape(bsz, N_HEADS, 2, I_LO, HEAD).transpose(4, 3, 2, 0, 1)
            s0 = s0.reshape(HEAD, I_LO, 2 * HEAD)
            a_t = _to_scan_keys(a)
            a_next = jnp.concatenate([a_t[1:], jnp.zeros_like(a_t[:1])], axis=0)
            y_t, s_fin = _wkv_scan_call(a_next, _to_scan_keys(w), _to_scan_keys(b), _to_scan_keys(k),
                                        _to_scan_keys(r), _to_scan_values(v), a_t[0], s0, tt)
            ya = y_t.reshape(t_len, I_LO, 2, bsz, N_HEADS).transpose(0, 3, 4, 2, 1).reshape(t_len, bsz, C_A)
            s_new = s_fin.reshape(HEAD, I_LO, 2, bsz, N_HEADS).transpose(3, 4, 2, 1, 0)
            s_new = s_new.reshape(bsz, N_HEADS, HEAD, HEAD)
        else:
            shp = (bsz, N_HEADS, 1, HEAD)
            y4, s_new = _wkv_step_call(a.reshape(shp), w.reshape(shp), b.reshape(shp), k.reshape(shp),
                                       r.reshape(shp), v.reshape(shp), st_wkv[l], 16)
            ya = y4.reshape(1, bsz, C_A)
        x1, h2, comb = _mix_post_call(ya, bonus, g, yb, x, bm_in, tt, g1, sh2, sc2, wl)
        bm_out = batch_major and l == DEPTH - 1
        x = _moe_call(h2, comb, x1, bm_out, moe_tt, g2, wl)
        wkv_out.append(s_new)
        shift_out.append(shift_new)
        conv_out.append(jnp.swapaxes(conv_new, 0, 1))
        lru_out.append(lru_new)
    return x, jnp.stack(wkv_out), jnp.stack(shift_out), jnp.stack(conv_out), jnp.stack(lru_out)


def kernel(x_prompt, x_sample, c_prompt, c_sample, state_wkv, state_shift, state_conv, state_lru, w_ada, b_ada, w_in, mu_shift, w0, w_decay_up, a0, w_a_up, w_g_up, k_k, k_a, r_k, lnx_gain, lnx_bias, conv_w, conv_b, w_rgate, b_rgate, w_igate, b_igate, lru_lambda, w_out, ln1_gain, ln1_bias, w_router_group, b_router_group, w_router_expert, b_router_expert, w_exp_gate, w_exp_up, w_exp_down, ln2_gain, ln2_bias):
    p = dict(w_in=w_in, mu_shift=mu_shift, w0=w0, w_decay_up=w_decay_up, a0=a0, w_a_up=w_a_up,
             w_g_up=w_g_up, k_k=k_k, k_a=k_a, r_k=r_k.reshape(DEPTH, C_A), lnx_gain=lnx_gain,
             lnx_bias=lnx_bias, conv_w=conv_w, conv_b=conv_b, w_rgate=w_rgate, b_rgate=b_rgate,
             w_igate=w_igate, b_igate=b_igate, lru_lambda=lru_lambda, w_out=w_out, ln1_gain=ln1_gain,
             ln1_bias=ln1_bias, w_router_group=w_router_group, b_router_group=b_router_group,
             w_router_expert=w_router_expert, b_router_expert=b_router_expert, w_exp_gate=w_exp_gate,
             w_exp_up=w_exp_up, w_exp_down=w_exp_down, ln2_gain=ln2_gain, ln2_bias=ln2_bias)
    layers = [_prep_layer(p, l, precise=(l == 0)) for l in range(DEPTH)]
    bp, bs = x_prompt.shape[0], x_sample.shape[0]

    mod = _ada_call(jnp.concatenate([c_sample, c_prompt], axis=0), w_ada, b_ada)
    split = lambda m: [m[:, i * D_MODEL:(i + 1) * D_MODEL] for i in range(6)]
    mods_s = [split(mod[l, :bs]) for l in range(DEPTH)]
    mods_p = [split(mod[l, bs:bs + bp]) for l in range(DEPTH)]

    z_wkv = jnp.zeros((DEPTH, bp, N_HEADS, HEAD, HEAD), F32)
    z_shift = jnp.zeros((DEPTH, bp, D_MODEL), F32)
    z_conv = jnp.zeros((DEPTH, bp, CONV_W - 1, C_B), F32)
    z_lru = jnp.zeros((DEPTH, bp, C_B), F32)
    y_p, wkv_p, shift_p, conv_p, lru_p = _trunk(x_prompt, True, 64, mods_p, z_wkv, z_shift, z_conv, z_lru,
                                                layers, 128)
    xs_tm = x_sample.reshape(1, bs, D_MODEL)
    y_s, wkv_s, shift_s, conv_s, lru_s = _trunk(xs_tm, False, 1, mods_s, state_wkv, state_shift, state_conv,
                                                state_lru, layers, 1)
    return (y_p, y_s.reshape(bs, 1, D_MODEL), wkv_p, shift_p, conv_p, lru_p, wkv_s, shift_s, conv_s, lru_s)
```

```python
import functools

import jax
import jax.numpy as jnp
from jax import lax
from jax.experimental import pallas as pl
from jax.experimental.pallas import tpu as pltpu

F32 = jnp.float32
BF16 = jnp.bfloat16

D_MODEL = 1024
DEPTH = 2
C_A = 512
C_B = 512
HEAD = 64
N_HEADS = C_A // HEAD
LORA_DECAY = 32
LORA_AAA = 32
LORA_GATE = 96
LORA_PAD = 128
N_SHIFT = 3 * C_A + LORA_DECAY + LORA_AAA + LORA_GATE
N_SHIFT_PAD = 3 * C_A + 3 * LORA_PAD
N_IN_PAD = N_SHIFT_PAD + 2 * C_B
CONV_W = 4
LRU_BLOCKS = 8
LRU_C = 8.0
N_GROUPS = 4
EXP_PER_GROUP = 4
N_EXPERTS = 16
D_EXPERT = 256
ROUTER_LANES = 128
ALPHA = (2 * DEPTH) ** 0.25
LN_EPS = 1e-5
GN_EPS = 64e-5
J_HALF = HEAD // 2
VMEM_LIMIT = 56 * 1024 * 1024


def _params(n_axes=1):
    return pltpu.CompilerParams(dimension_semantics=("arbitrary",) * n_axes,
                                vmem_limit_bytes=VMEM_LIMIT)


def _full(shape):
    return pl.BlockSpec(shape, lambda *_: (0,) * len(shape))


def _const(shape):
    return pl.BlockSpec(shape, lambda *_: (0,) * len(shape), pipeline_mode=pl.Buffered(1))


def _layer_norm(x):
    mu = jnp.mean(x, axis=-1, keepdims=True)
    xc = x - mu
    var = jnp.mean(xc * xc, axis=-1, keepdims=True)
    return xc * lax.rsqrt(var + LN_EPS)


def _softplus(z):
    return jnp.maximum(z, 0.0) + jnp.log1p(jnp.exp(-jnp.abs(z)))


def _sigmoid(z):
    return 1.0 / (1.0 + jnp.exp(-z))


def _gelu_tanh(x):
    c = 0.7978845608028654
    return x * (0.5 * (1.0 + jnp.tanh(c * (x + 0.044715 * (x * x * x)))))


def _dot(a, b):
    return jnp.dot(a, b, preferred_element_type=F32)


def _split(x, precise=True):
    hi = x.astype(BF16)
    return hi, ((x - hi.astype(F32)).astype(BF16) if precise else None)


def _mm(xs, w_hi, w_lo=None):
    hi, lo = xs
    out = _dot(hi, w_hi)
    if w_lo is not None:
        out = out + _dot(lo, w_hi) + _dot(hi, w_lo)
    return out


def _split_dot(x, w_bf16):
    hi, lo = _split(x)
    return _dot(hi, w_bf16) + _dot(lo, w_bf16)


def _load_time_major(x_ref, tm_ref, batch_major, tt):
    if not batch_major:
        return x_ref[...]
    for t in range(tt):
        tm_ref[t] = x_ref[:, t, :]
    return tm_ref[...]


def _store_batch_major(o_ref, val, tt):
    for t in range(tt):
        o_ref[:, t, :] = val[t]


def _ada_kernel(c_ref, w_ref, b_ref, o_ref):
    c = c_ref[...]
    w_hi, w_lo = _split(w_ref[0])
    o_ref[0] = _mm(_split(c * _sigmoid(c)), w_hi, w_lo) + b_ref[0]


def _ada_call(c_all, w_ada, b_ada):
    n = c_all.shape[0]
    tn = 1536
    return pl.pallas_call(
        _ada_kernel,
        grid=(DEPTH, 6 * D_MODEL // tn),
        in_specs=[pl.BlockSpec((n, D_MODEL), lambda l, j: (0, 0)),
                  pl.BlockSpec((1, D_MODEL, tn), lambda l, j: (l, 0, j)),
                  pl.BlockSpec((1, 1, tn), lambda l, j: (l, 0, j))],
        out_specs=pl.BlockSpec((1, n, tn), lambda l, j: (l, 0, j)),
        out_shape=jax.ShapeDtypeStruct((DEPTH, n, 6 * D_MODEL), F32),
        compiler_params=_params(2),
        name="ada_mod",
    )(c_all, w_ada, b_ada.reshape(DEPTH, 1, 6 * D_MODEL))


N_PRE_MATS = 5


def _mix_pre_kernel(precise, batch_major, tt, bsz,
                    x_ref, sh_ref, sc_ref, hprev_ref, cst_ref, lst_ref, *refs):
    n_mat = N_PRE_MATS * (2 if precise else 1)
    mats = refs[:n_mat]
    if precise:
        (win_ref, win_lo), (wdu_ref, wdu_lo), (wau_ref, wau_lo), (wgu_ref, wgu_lo), (wg_ref, wg_lo) = (
            (mats[2 * i], mats[2 * i + 1]) for i in range(N_PRE_MATS))
    else:
        (win_ref, win_lo), (wdu_ref, wdu_lo), (wau_ref, wau_lo), (wgu_ref, wgu_lo), (wg_ref, wg_lo) = (
            (m, None) for m in mats)
    (mu_ref, w0_ref, a0_ref, kk_ref, ka_ref, rk_ref, cw_ref, cb_ref, bg_ref, lam_ref, gones_ref,
     r_o, w_o, k_o, v_o, a_o, b_o, g_o, bonus_o, yb_o, shift_o, conv_o, lru_o,
     prevp_s, conv_s, lru_s, a_s, u_s, hs_s, *tm_s) = refs[n_mat:]
    rows = tt * bsz

    def win(lo_hi_ref, c0, c1):
        return None if lo_hi_ref is None else lo_hi_ref[:, c0:c1]

    @pl.when(pl.program_id(0) == 0)
    def _():
        prevp_s[...] = _mm(_split(hprev_ref[...], precise), win_ref[:, :N_SHIFT_PAD], win(win_lo, 0, N_SHIFT_PAD))
        conv_s[...] = cst_ref[...]
        lru_s[...] = lst_ref[...]

    x = _load_time_major(x_ref, tm_s[0] if batch_major else None, batch_major, tt)
    h = _layer_norm(x) * (1.0 + sc_ref[...]) + sh_ref[...]
    shift_o[...] = h[tt - 1]
    hb = _split(h.reshape(rows, D_MODEL), precise)

    def shifted(off, width):
        p = _mm(hb, win_ref[:, off:off + width], win(win_lo, off, off + width)).reshape(tt, bsz, width)
        first = prevp_s[:, off:off + width][None]
        prev = jnp.concatenate([first, p[:tt - 1]], axis=0) if tt > 1 else first
        prevp_s[:, off:off + width] = p[tt - 1]
        return p + (prev - p) * mu_ref[:, off:off + width]

    def up(z, w_ref_, w_lo_):
        zs = _split(z.reshape(rows, LORA_PAD), precise)
        return _mm(zs, w_ref_[...], None if w_lo_ is None else w_lo_[...]).reshape(tt, bsz, C_A)

    def gsum(z):
        return _split_dot(z.reshape(rows, C_A), gones_ref[...]).reshape(tt, bsz, C_A)

    r = shifted(0, C_A)
    k = shifted(C_A, C_A)
    v = shifted(2 * C_A, C_A)
    lora = shifted(3 * C_A, 3 * LORA_PAD)
    wd = lora[:, :, 0:LORA_PAD]
    ad = lora[:, :, LORA_PAD:2 * LORA_PAD]
    gd = lora[:, :, 2 * LORA_PAD:3 * LORA_PAD]

    w_log = -_softplus(-(w0_ref[...] + up(jnp.tanh(wd), wdu_ref, wdu_lo))) - 0.5
    decay = jnp.exp(-jnp.exp(w_log))
    a_sig = _sigmoid(a0_ref[...] + up(ad, wau_ref, wau_lo))
    g_o[...] = up(_sigmoid(gd), wgu_ref, wgu_lo)

    kk = k * kk_ref[...]
    kk = kk / jnp.maximum(jnp.sqrt(gsum(kk * kk)), 1e-12)
    k2 = k * (1.0 + (a_sig - 1.0) * ka_ref[...])
    for o_ref, val in ((r_o, r), (w_o, decay), (k_o, k2), (v_o, v), (a_o, -kk), (b_o, kk * a_sig)):
        _store_batch_major(o_ref, val, tt)
    bonus_o[...] = gsum(r * k2 * rk_ref[...]) * v

    g0, g1 = N_SHIFT_PAD, N_SHIFT_PAD + C_B
    gate_b = _mm(hb, win_ref[:, g0:g1], win(win_lo, g0, g1)).reshape(tt, bsz, C_B)
    rec_b = _mm(hb, win_ref[:, g1:N_IN_PAD], win(win_lo, g1, N_IN_PAD)).reshape(tt, bsz, C_B)
    xp = jnp.concatenate([conv_s[...], rec_b], axis=0)
    xc = cb_ref[...] + xp[0:tt] * cw_ref[0:1, :]
    for j in range(1, CONV_W):
        xc = xc + xp[j:j + tt] * cw_ref[j:j + 1, :]
    conv_new = xp[tt:tt + CONV_W - 1]
    conv_s[...] = conv_new
    conv_o[...] = conv_new

    gates = _mm(_split(xc.reshape(rows, C_B), precise), wg_ref[...],
                None if wg_lo is None else wg_lo[...]) + bg_ref[...]
    r_t = _sigmoid(gates[:, :C_B]).reshape(tt, bsz, C_B)
    i_t = _sigmoid(gates[:, C_B:]).reshape(tt, bsz, C_B)
    log_a = -LRU_C * r_t * _softplus(-lam_ref[...])
    a_s[...] = jnp.exp(log_a)
    u_s[...] = jnp.sqrt(1.0 - jnp.exp(2.0 * log_a)) * (i_t * xc)

    def scan_step(t, hcur):
        hcur = a_s[t] * hcur + u_s[t]
        hs_s[t] = hcur
        return hcur

    h_fin = lax.fori_loop(0, tt, scan_step, lru_s[...])
    lru_s[...] = h_fin
    lru_o[...] = h_fin
    yb_o[...] = hs_s[...] * _gelu_tanh(gate_b)


def _mix_pre_call(x, batch_major, tt, sh1, sc1, h_prev, conv_state, lru_state, wl):
    precise = wl['precise']
    if batch_major:
        bsz, t_len, _ = x.shape
        x_spec = pl.BlockSpec((bsz, tt, D_MODEL), lambda c: (0, c, 0))
    else:
        t_len, bsz, _ = x.shape
        x_spec = pl.BlockSpec((tt, bsz, D_MODEL), lambda c: (c, 0, 0))
    seq = lambda ch: pl.BlockSpec((tt, bsz, ch), lambda c: (c, 0, 0))
    weights = []
    for name in ('w_in', 'w_decay_up', 'w_a_up', 'w_g_up', 'w_gates'):
        weights.append(wl[name])
        if precise:
            weights.append(wl[name + '_lo'])
    weights += [wl['mu'], wl['w0'], wl['a0'], wl['k_k'], wl['k_a'], wl['r_k'], wl['conv_w'], wl['conv_b'],
                wl['b_gates'], wl['lam'], wl['gones']]
    seq_out = jax.ShapeDtypeStruct((t_len, bsz, C_A), F32)
    scan_spec = pl.BlockSpec((bsz, tt, C_A), lambda c: (0, c, 0))
    scan_out = jax.ShapeDtypeStruct((bsz, t_len, C_A), F32)
    scratch = [pltpu.VMEM((bsz, N_SHIFT_PAD), F32), pltpu.VMEM((CONV_W - 1, bsz, C_B), F32),
               pltpu.VMEM((bsz, C_B), F32), pltpu.VMEM((tt, bsz, C_B), F32),
               pltpu.VMEM((tt, bsz, C_B), F32), pltpu.VMEM((tt, bsz, C_B), F32)]
    if batch_major:
        scratch.append(pltpu.VMEM((tt, bsz, D_MODEL), F32))
    return pl.pallas_call(
        functools.partial(_mix_pre_kernel, precise, batch_major, tt, bsz),
        grid=(t_len // tt,),
        in_specs=[x_spec, _full((bsz, D_MODEL)), _full((bsz, D_MODEL)), _full((bsz, D_MODEL)),
                  _full((CONV_W - 1, bsz, C_B)), _full((bsz, C_B))] + [_const(w.shape) for w in weights],
        out_specs=[scan_spec] * 6 + [seq(C_A)] * 3 + [_full((bsz, D_MODEL)), _full((CONV_W - 1, bsz, C_B)),
                                                      _full((bsz, C_B))],
        out_shape=[scan_out] * 6 + [seq_out] * 3 + [jax.ShapeDtypeStruct((bsz, D_MODEL), F32),
                                   jax.ShapeDtypeStruct((CONV_W - 1, bsz, C_B), F32),
                                   jax.ShapeDtypeStruct((bsz, C_B), F32)],
        scratch_shapes=scratch,
        compiler_params=_params(),
        name="mix_pre",
    )(x, sh1, sc1, h_prev, conv_state, lru_state, *weights)


I_LO = HEAD // 2
SUB = 8
N_STRIP = I_LO // SUB


def _strips(ref, *lead):
    return [ref[(*lead, pl.ds(q * SUB, SUB), slice(None))] for q in range(N_STRIP)]


def _row(ref, *idx):
    return jnp.broadcast_to(ref[(*idx, slice(None))], (SUB, 2 * HEAD))


A_OP, W_OP, B_OP, K_OP, R_OP = range(5)
RELAYOUT_UNROLL = 8


def _wkv_scan_kernel(tt, a_ref, w_ref, b_ref, k_ref, r_ref, v_ref, s0_ref,
                     y_ref, sfin_ref, s_s, ops_s, zt_s, vop_s, vt_s, ysc_s):
    bsz = a_ref.shape[0]
    inst = bsz * N_HEADS

    @pl.when(pl.program_id(0) == 0)
    def _():
        s_s[...] = s0_ref[...]

    for n, raw in enumerate((a_ref, w_ref, b_ref, k_ref, r_ref)):
        for bi in range(bsz):
            zt_s[bi * N_HEADS:(bi + 1) * N_HEADS] = raw[bi].T.reshape(N_HEADS, HEAD, tt)

        def key_rows(j, carry, n=n):
            rows = zt_s[:, j, :]
            ops_s[n, j] = jnp.concatenate([rows, rows], axis=0).T
            return carry

        lax.fori_loop(0, HEAD, key_rows, 0, unroll=RELAYOUT_UNROLL)

    for bi in range(bsz):
        vt_s[bi * N_HEADS:(bi + 1) * N_HEADS] = v_ref[bi].T.reshape(N_HEADS, 2, I_LO, tt)

    def value_rows(il, carry):
        vop_s[:, il, :] = jnp.concatenate([vt_s[:, 0, il, :], vt_s[:, 1, il, :]], axis=0).T
        return carry

    lax.fori_loop(0, I_LO, value_rows, 0, unroll=RELAYOUT_UNROLL)

    sa = [None] * N_STRIP
    for j in range(HEAD):
        a8 = _row(ops_s, A_OP, j, pl.ds(0, 1))
        for q, s in enumerate(_strips(s_s, j)):
            sa[q] = s * a8 if sa[q] is None else sa[q] + s * a8

    def step(t, sa):
        t_next = jnp.minimum(t + 1, tt - 1)
        v = _strips(vop_s, t)
        y = [None] * N_STRIP
        sa_next = [None] * N_STRIP
        for j in range(HEAD):
            w8, b8, k8, r8 = (_row(ops_s, n, j, pl.ds(t, 1)) for n in (W_OP, B_OP, K_OP, R_OP))
            a8 = _row(ops_s, A_OP, j, pl.ds(t_next, 1))
            for q in range(N_STRIP):
                rows = pl.ds(q * SUB, SUB)
                s_new = s_s[j, rows, :] * w8 + sa[q] * b8 + v[q] * k8
                s_s[j, rows, :] = s_new
                yq, aq = s_new * r8, s_new * a8
                y[q] = yq if y[q] is None else y[q] + yq
                sa_next[q] = aq if sa_next[q] is None else sa_next[q] + aq
        for q in range(N_STRIP):
            ysc_s[t, pl.ds(q * SUB, SUB), :] = y[q]
        return tuple(sa_next)

    lax.fori_loop(0, tt, step, tuple(sa))

    def y_rows(il, carry):
        yt = ysc_s[:, il, :].T
        vt_s[:, 0, il, :] = yt[:inst]
        vt_s[:, 1, il, :] = yt[inst:]
        return carry

    lax.fori_loop(0, I_LO, y_rows, 0, unroll=RELAYOUT_UNROLL)
    for bi in range(bsz):
        y_ref[bi] = vt_s[bi * N_HEADS:(bi + 1) * N_HEADS].reshape(C_A, tt).T

    @pl.when(pl.program_id(0) == pl.num_programs(0) - 1)
    def _():
        sfin_ref[...] = s_s[...]


def _wkv_scan_call(a, w, b, k, r, v, s0, tt):
    bsz, t_len, _ = w.shape
    inst = bsz * N_HEADS
    raw = pl.BlockSpec((bsz, tt, C_A), lambda c: (0, c, 0), pipeline_mode=pl.Buffered(1))
    state = _const((HEAD, I_LO, 2 * HEAD))
    return pl.pallas_call(
        functools.partial(_wkv_scan_kernel, tt),
        grid=(t_len // tt,),
        in_specs=[raw] * 6 + [state],
        out_specs=[pl.BlockSpec((bsz, tt, C_A), lambda c: (0, c, 0)), _full((HEAD, I_LO, 2 * HEAD))],
        out_shape=[jax.ShapeDtypeStruct((bsz, t_len, C_A), F32),
                   jax.ShapeDtypeStruct((HEAD, I_LO, 2 * HEAD), F32)],
        scratch_shapes=[pltpu.VMEM((HEAD, I_LO, 2 * HEAD), F32),
                        pltpu.VMEM((5, HEAD, tt, 2 * inst), F32),
                        pltpu.VMEM((inst, HEAD, tt), F32),
                        pltpu.VMEM((tt, I_LO, 2 * inst), F32),
                        pltpu.VMEM((inst, 2, I_LO, tt), F32),
                        pltpu.VMEM((tt, I_LO, 2 * inst), F32)],
        compiler_params=_params(),
        name="wkv_scan",
    )(a, w, b, k, r, v, s0)


def _wkv_step_kernel(a_ref, w_ref, b_ref, k_ref, r_ref, v_ref, s_ref, y_ref, so_ref):
    s = s_ref[...]
    row = lax.broadcasted_iota(jnp.int32, (HEAD, HEAD), 0)
    col = lax.broadcasted_iota(jnp.int32, (HEAD, HEAD), 1)
    eye = (row == col).astype(F32)
    sa = jnp.sum(s * a_ref[...], axis=-1, keepdims=True)
    v_col = jnp.sum(eye * v_ref[...], axis=-1, keepdims=True)
    s_new = s * w_ref[...] + sa * b_ref[...] + v_col * k_ref[...]
    so_ref[...] = s_new
    y_col = jnp.sum(s_new * r_ref[...], axis=-1, keepdims=True)
    y_ref[...] = jnp.sum(eye * y_col, axis=-2, keepdims=True)


def _wkv_step_call(a, w, b, k, r, v, s, bb):
    bsz = s.shape[0]
    op = pl.BlockSpec((bb, N_HEADS, 1, HEAD), lambda c: (c, 0, 0, 0))
    st = pl.BlockSpec((bb, N_HEADS, HEAD, HEAD), lambda c: (c, 0, 0, 0))
    return pl.pallas_call(
        _wkv_step_kernel,
        grid=(bsz // bb,),
        in_specs=[op] * 6 + [st],
        out_specs=[op, st],
        out_shape=[jax.ShapeDtypeStruct((bsz, N_HEADS, 1, HEAD), F32),
                   jax.ShapeDtypeStruct((bsz, N_HEADS, HEAD, HEAD), F32)],
        compiler_params=_params(),
        name="wkv_step",
    )(a, w, b, k, r, v, s)


def _route(logits):
    lane = lax.broadcasted_iota(jnp.int32, logits.shape, 1)
    neg = jnp.float32(-jnp.inf)
    big = jnp.int32(ROUTER_LANES)
    is_grp = (lane >= N_EXPERTS) & (lane < N_EXPERTS + N_GROUPS)
    gl = jnp.where(is_grp, logits, neg)
    gmax = jnp.max(gl, axis=-1, keepdims=True)
    ge = jnp.where(is_grp, jnp.exp(gl - gmax), 0.0)
    gp = ge / jnp.sum(ge, axis=-1, keepdims=True)
    gi = jnp.min(jnp.where(gl == gmax, lane, big), axis=-1, keepdims=True)
    p_grp = jnp.sum(jnp.where(lane == gi, gp, 0.0), axis=-1, keepdims=True)
    gidx = gi - N_EXPERTS
    in_grp = (lane >= gidx * EXP_PER_GROUP) & (lane < (gidx + 1) * EXP_PER_GROUP)
    el = jnp.where(in_grp, logits, neg)
    emax = jnp.max(el, axis=-1, keepdims=True)
    ee = jnp.where(in_grp, jnp.exp(el - emax), 0.0)
    pe = ee / jnp.sum(ee, axis=-1, keepdims=True)
    pe_m = jnp.where(in_grp, pe, -1.0)
    v1 = jnp.max(pe_m, axis=-1, keepdims=True)
    i1 = jnp.min(jnp.where(pe_m == v1, lane, big), axis=-1, keepdims=True)
    pe_m2 = jnp.where(lane == i1, -1.0, pe_m)
    v2 = jnp.max(pe_m2, axis=-1, keepdims=True)
    i2 = jnp.min(jnp.where(pe_m2 == v2, lane, big), axis=-1, keepdims=True)
    tot = v1 + v2
    return jnp.where(lane == i1, v1 / tot * p_grp, 0.0) + jnp.where(lane == i2, v2 / tot * p_grp, 0.0)


def _mix_post_kernel(precise, batch_major, tt, bsz,
                     ya_ref, bonus_ref, g_ref, yb_ref, x_ref, g1_ref, sh2_ref, sc2_ref, wout_ref, *refs):
    wout_lo = refs[0] if precise else None
    (lnxg_ref, lnxb_ref, gones_ref, ln1g_ref, ln1b_ref, wrh_ref, wrl_ref, br_ref,
     x1_o, h2_o, comb_o, ya_tm_s, *tm_s) = refs[1:] if precise else refs
    rows = tt * bsz

    def wout(lo_hi_ref, r0, r1):
        return None if lo_hi_ref is None else lo_hi_ref[r0:r1, :]

    def gsum(z):
        return _split_dot(z.reshape(rows, C_A), gones_ref[...]).reshape(tt, bsz, C_A)

    ya = _load_time_major(ya_ref, ya_tm_s, True, tt)
    mu = gsum(ya) * (1.0 / HEAD)
    yc = ya - mu
    var = gsum(yc * yc) * (1.0 / HEAD)
    yn = yc * lax.rsqrt(var + GN_EPS) * lnxg_ref[...] + lnxb_ref[...]
    ya = (yn + bonus_ref[...]) * g_ref[...]
    y = (_mm(_split(ya.reshape(rows, C_A), precise), wout_ref[0:C_A, :], wout(wout_lo, 0, C_A))
         + _mm(_split(yb_ref[...].reshape(rows, C_B), precise), wout_ref[C_A:, :],
               wout(wout_lo, C_A, C_A + C_B))).reshape(tt, bsz, D_MODEL)
    x = _load_time_major(x_ref, tm_s[0] if batch_major else None, batch_major, tt)
    x1 = _layer_norm(ALPHA * x + (1.0 + g1_ref[...]) * y) * ln1g_ref[...] + ln1b_ref[...]
    x1_o[...] = x1
    h2 = (_layer_norm(x1) * (1.0 + sc2_ref[...]) + sh2_ref[...]).reshape(rows, D_MODEL)
    hi = h2.astype(BF16)
    lo = (h2 - hi.astype(F32)).astype(BF16)
    h2_o[...] = hi
    logits = _dot(hi, wrh_ref[...]) + _dot(lo, wrh_ref[...]) + _dot(hi, wrl_ref[...]) + br_ref[...]
    comb_o[...] = _route(logits)


def _mix_post_call(ya, bonus, g, yb, x, batch_major, tt, g1, sh2, sc2, wl):
    bsz, t_len, _ = ya.shape
    rows = tt * bsz
    ya_spec = pl.BlockSpec((bsz, tt, C_A), lambda c: (0, c, 0))
    if batch_major:
        x_spec = pl.BlockSpec((bsz, tt, D_MODEL), lambda c: (0, c, 0))
    else:
        x_spec = pl.BlockSpec((tt, bsz, D_MODEL), lambda c: (c, 0, 0))
    seq = pl.BlockSpec((tt, bsz, C_A), lambda c: (c, 0, 0))
    precise = wl['precise']
    weights = [wl['w_out']] + ([wl['w_out_lo']] if precise else []) + [
        wl['lnx_gain'], wl['lnx_bias'], wl['gones'], wl['ln1_gain'], wl['ln1_bias'],
        wl['w_router_hi'], wl['w_router_lo'], wl['b_router']]
    scratch = [pltpu.VMEM((tt, bsz, C_A), F32)] + ([pltpu.VMEM((tt, bsz, D_MODEL), F32)] if batch_major else [])
    return pl.pallas_call(
        functools.partial(_mix_post_kernel, precise, batch_major, tt, bsz),
        grid=(t_len // tt,),
        in_specs=[ya_spec] + [seq] * 3 + [x_spec] + [_full((bsz, D_MODEL))] * 3 + [_const(w.shape) for w in weights],
        out_specs=[pl.BlockSpec((tt, bsz, D_MODEL), lambda c: (c, 0, 0)),
                   pl.BlockSpec((rows, D_MODEL), lambda c: (c, 0)),
                   pl.BlockSpec((rows, ROUTER_LANES), lambda c: (c, 0))],
        out_shape=[jax.ShapeDtypeStruct((t_len, bsz, D_MODEL), F32),
                   jax.ShapeDtypeStruct((t_len * bsz, D_MODEL), BF16),
                   jax.ShapeDtypeStruct((t_len * bsz, ROUTER_LANES), F32)],
        scratch_shapes=scratch,
        compiler_params=_params(),
        name="mix_post",
    )(ya, bonus, g, yb, x, g1, sh2, sc2, *weights)


def _moe_kernel(batch_major_out, tt, bsz,
                h2_ref, comb_ref, x1_ref, g2_ref, wg_ref, wu_ref, wd_ref, ln2g_ref, ln2b_ref,
                o_ref, acc_s):
    e = pl.program_id(1)
    rows = tt * bsz

    @pl.when(e == 0)
    def _():
        acc_s[...] = jnp.zeros_like(acc_s)

    h = h2_ref[...]
    lane = lax.broadcasted_iota(jnp.int32, (rows, ROUTER_LANES), 1)
    ce = jnp.sum(jnp.where(lane == e, comb_ref[...], 0.0), axis=-1, keepdims=True)
    gate = _dot(h, wg_ref[0])
    hid = (gate * _sigmoid(gate)) * _dot(h, wu_ref[0])
    acc_s[...] += _dot((hid * ce).astype(BF16), wd_ref[0])

    @pl.when(e == N_EXPERTS - 1)
    def _():
        moe = acc_s[...].reshape(tt, bsz, D_MODEL)
        out = _layer_norm(ALPHA * x1_ref[...] + (1.0 + g2_ref[...]) * moe) * ln2g_ref[...] + ln2b_ref[...]
        if batch_major_out:
            for t in range(tt):
                o_ref[:, t, :] = out[t]
        else:
            o_ref[...] = out


def _moe_call(h2, comb, x1, batch_major_out, tt, g2, wl):
    t_len, bsz, _ = x1.shape
    rows = tt * bsz
    if batch_major_out:
        o_spec = pl.BlockSpec((bsz, tt, D_MODEL), lambda c, e: (0, c, 0))
        o_shape = jax.ShapeDtypeStruct((bsz, t_len, D_MODEL), F32)
    else:
        o_spec = pl.BlockSpec((tt, bsz, D_MODEL), lambda c, e: (c, 0, 0))
        o_shape = jax.ShapeDtypeStruct((t_len, bsz, D_MODEL), F32)
    return pl.pallas_call(
        functools.partial(_moe_kernel, batch_major_out, tt, bsz),
        grid=(t_len // tt, N_EXPERTS),
        in_specs=[pl.BlockSpec((rows, D_MODEL), lambda c, e: (c, 0)),
                  pl.BlockSpec((rows, ROUTER_LANES), lambda c, e: (c, 0)),
                  pl.BlockSpec((tt, bsz, D_MODEL), lambda c, e: (c, 0, 0)),
                  pl.BlockSpec((bsz, D_MODEL), lambda c, e: (0, 0)),
                  pl.BlockSpec((1, D_MODEL, D_EXPERT), lambda c, e: (e, 0, 0)),
                  pl.BlockSpec((1, D_MODEL, D_EXPERT), lambda c, e: (e, 0, 0)),
                  pl.BlockSpec((1, D_EXPERT, D_MODEL), lambda c, e: (e, 0, 0)),
                  pl.BlockSpec((1, D_MODEL), lambda c, e: (0, 0)),
                  pl.BlockSpec((1, D_MODEL), lambda c, e: (0, 0))],
        out_specs=o_spec,
        out_shape=o_shape,
        scratch_shapes=[pltpu.VMEM((rows, D_MODEL), F32)],
        compiler_params=_params(2),
        name="moe",
    )(h2, comb, x1, g2, wl['w_exp_gate'], wl['w_exp_up'], wl['w_exp_down'], wl['ln2_gain'], wl['ln2_bias'])


def _pad_cols(w, width):
    return jnp.pad(w, ((0, 0), (0, width - w.shape[1])))


def _pad_rows(w, height):
    return jnp.pad(w, ((0, height - w.shape[0]), (0, 0)))


def _block_diag(w):
    n, c, d = w.shape
    eye = jnp.eye(n, dtype=w.dtype)
    return (eye[:, None, :, None] * w[:, :, None, :]).reshape(n * c, n * d)


def _prep_layer(p, l, precise):
    row = lambda v: v[l].reshape(1, -1)
    w_in = p['w_in'][l]
    o1, o2, o3 = 3 * C_A, 3 * C_A + LORA_DECAY, 3 * C_A + LORA_DECAY + LORA_AAA
    pieces = [(0, o1, o1), (o1, o2, LORA_PAD), (o2, o3, LORA_PAD), (o3, N_SHIFT, LORA_PAD)]
    w_in_p = jnp.concatenate([_pad_cols(w_in[:, a:b], wd) for a, b, wd in pieces] + [w_in[:, N_SHIFT:]], axis=1)
    mu = p['mu_shift'][l].reshape(1, -1)
    mu_p = jnp.concatenate([_pad_cols(mu[:, a:b], wd) for a, b, wd in pieces], axis=1)
    w_router = jnp.concatenate([p['w_router_expert'][l], p['w_router_group'][l]], axis=1)
    w_router = _pad_cols(w_router, ROUTER_LANES)
    w_router_hi = w_router.astype(BF16)
    b_router = _pad_cols(jnp.concatenate([p['b_router_expert'][l], p['b_router_group'][l]]).reshape(1, -1),
                         ROUTER_LANES)
    head_id = jnp.arange(C_A) // HEAD
    mats = dict(
        w_in=w_in_p,
        w_decay_up=_pad_rows(p['w_decay_up'][l], LORA_PAD),
        w_a_up=_pad_rows(p['w_a_up'][l], LORA_PAD),
        w_g_up=_pad_rows(p['w_g_up'][l], LORA_PAD),
        w_gates=jnp.concatenate([_block_diag(p['w_rgate'][l]), _block_diag(p['w_igate'][l])], axis=1),
        w_out=p['w_out'][l])
    split_mats = {}
    for name, w in mats.items():
        hi = w.astype(BF16)
        split_mats[name] = hi
        if precise:
            split_mats[name + '_lo'] = (w - hi.astype(F32)).astype(BF16)
    return dict(
        split_mats, precise=precise, mu=mu_p, w0=row(p['w0']), a0=row(p['a0']),
        k_k=row(p['k_k']), k_a=row(p['k_a']), r_k=row(p['r_k']),
        conv_w=p['conv_w'][l], conv_b=row(p['conv_b']),
        b_gates=jnp.concatenate([p['b_rgate'][l], p['b_igate'][l]]).reshape(1, -1),
        lam=row(p['lru_lambda']),
        gones=(head_id[:, None] == head_id[None, :]).astype(BF16),
        lnx_gain=row(p['lnx_gain']), lnx_bias=row(p['lnx_bias']),
        ln1_gain=row(p['ln1_gain']), ln1_bias=row(p['ln1_bias']),
        w_router_hi=w_router_hi, w_router_lo=(w_router - w_router_hi.astype(F32)).astype(BF16),
        b_router=b_router,
        w_exp_gate=p['w_exp_gate'][l].astype(BF16), w_exp_up=p['w_exp_up'][l].astype(BF16),
        w_exp_down=p['w_exp_down'][l].astype(BF16),
        ln2_gain=row(p['ln2_gain']), ln2_bias=row(p['ln2_bias']),
    )


def _trunk(x, batch_major, tt, mods, st_wkv, st_shift, st_conv, st_lru, layers, moe_tt, scan_tt):
    wkv_out, shift_out, conv_out, lru_out = [], [], [], []
    for l in range(DEPTH):
        wl = layers[l]
        sh1, sc1, g1, sh2, sc2, g2 = mods[l]
        bm_in = batch_major and l == 0
        conv_state = jnp.swapaxes(st_conv[l], 0, 1)
        (r, w, k, v, a, b, g, bonus, yb, shift_new, conv_new, lru_new) = _mix_pre_call(
            x, bm_in, tt, sh1, sc1, st_shift[l], conv_state, st_lru[l], wl)
        bsz, t_len, _ = r.shape
        if t_len > 1:
            s0 = st_wkv[l].reshape(bsz, N_HEADS, 2, I_LO, HEAD).transpose(4, 3, 2, 0, 1)
            s0 = s0.reshape(HEAD, I_LO, 2 * HEAD)
            ya, s_fin = _wkv_scan_call(a, w, b, k, r, v, s0, scan_tt)
            s_new = s_fin.reshape(HEAD, I_LO, 2, bsz, N_HEADS).transpose(3, 4, 2, 1, 0)
            s_new = s_new.reshape(bsz, N_HEADS, HEAD, HEAD)
        else:
            shp = (bsz, N_HEADS, 1, HEAD)
            y4, s_new = _wkv_step_call(a.reshape(shp), w.reshape(shp), b.reshape(shp), k.reshape(shp),
                                       r.reshape(shp), v.reshape(shp), st_wkv[l], 16)
            ya = y4.reshape(bsz, 1, C_A)
        x1, h2, comb = _mix_post_call(ya, bonus, g, yb, x, bm_in, tt, g1, sh2, sc2, wl)
        bm_out = batch_major and l == DEPTH - 1
        x = _moe_call(h2, comb, x1, bm_out, moe_tt, g2, wl)
        wkv_out.append(s_new)
        shift_out.append(shift_new)
        conv_out.append(jnp.swapaxes(conv_new, 0, 1))
        lru_out.append(lru_new)
    return x, jnp.stack(wkv_out), jnp.stack(shift_out), jnp.stack(conv_out), jnp.stack(lru_out)


def kernel(x_prompt, x_sample, c_prompt, c_sample, state_wkv, state_shift, state_conv, state_lru, w_ada, b_ada, w_in, mu_shift, w0, w_decay_up, a0, w_a_up, w_g_up, k_k, k_a, r_k, lnx_gain, lnx_bias, conv_w, conv_b, w_rgate, b_rgate, w_igate, b_igate, lru_lambda, w_out, ln1_gain, ln1_bias, w_router_group, b_router_group, w_router_expert, b_router_expert, w_exp_gate, w_exp_up, w_exp_down, ln2_gain, ln2_bias):
    p = dict(w_in=w_in, mu_shift=mu_shift, w0=w0, w_decay_up=w_decay_up, a0=a0, w_a_up=w_a_up,
             w_g_up=w_g_up, k_k=k_k, k_a=k_a, r_k=r_k.reshape(DEPTH, C_A), lnx_gain=lnx_gain,
             lnx_bias=lnx_bias, conv_w=conv_w, conv_b=conv_b, w_rgate=w_rgate, b_rgate=b_rgate,
             w_igate=w_igate, b_igate=b_igate, lru_lambda=lru_lambda, w_out=w_out, ln1_gain=ln1_gain,
             ln1_bias=ln1_bias, w_router_group=w_router_group, b_router_group=b_router_group,
             w_router_expert=w_router_expert, b_router_expert=b_router_expert, w_exp_gate=w_exp_gate,
             w_exp_up=w_exp_up, w_exp_down=w_exp_down, ln2_gain=ln2_gain, ln2_bias=ln2_bias)
    layers = [_prep_layer(p, l, precise=(l == 0)) for l in range(DEPTH)]
    bp, bs = x_prompt.shape[0], x_sample.shape[0]

    mod = _ada_call(jnp.concatenate([c_sample, c_prompt], axis=0), w_ada, b_ada)
    split = lambda m: [m[:, i * D_MODEL:(i + 1) * D_MODEL] for i in range(6)]
    mods_s = [split(mod[l, :bs]) for l in range(DEPTH)]
    mods_p = [split(mod[l, bs:bs + bp]) for l in range(DEPTH)]

    z_wkv = jnp.zeros((DEPTH, bp, N_HEADS, HEAD, HEAD), F32)
    z_shift = jnp.zeros((DEPTH, bp, D_MODEL), F32)
    z_conv = jnp.zeros((DEPTH, bp, CONV_W - 1, C_B), F32)
    z_lru = jnp.zeros((DEPTH, bp, C_B), F32)
    y_p, wkv_p, shift_p, conv_p, lru_p = _trunk(x_prompt, True, 64, mods_p, z_wkv, z_shift, z_conv, z_lru,
                                                layers, 128, 128)
    xs_tm = x_sample.reshape(1, bs, D_MODEL)
    y_s, wkv_s, shift_s, conv_s, lru_s = _trunk(xs_tm, False, 1, mods_s, state_wkv, state_shift, state_conv,
                                                state_lru, layers, 1, 1)
    return (y_p, y_s.reshape(bs, 1, D_MODEL), wkv_p, shift_p, conv_p, lru_p, wkv_s, shift_s, conv_s, lru_s)
```

```python
import functools

import jax
import jax.numpy as jnp
from jax import lax
from jax.experimental import pallas as pl
from jax.experimental.pallas import tpu as pltpu

F32 = jnp.float32
BF16 = jnp.bfloat16

D_MODEL = 1024
DEPTH = 2
C_A = 512
C_B = 512
HEAD = 64
N_HEADS = C_A // HEAD
LORA_DECAY = 32
LORA_AAA = 32
LORA_GATE = 96
LORA_PAD = 128
N_SHIFT = 3 * C_A + LORA_DECAY + LORA_AAA + LORA_GATE
N_SHIFT_PAD = 3 * C_A + 3 * LORA_PAD
N_IN_PAD = N_SHIFT_PAD + 2 * C_B
CONV_W = 4
LRU_BLOCKS = 8
LRU_C = 8.0
N_GROUPS = 4
EXP_PER_GROUP = 4
N_EXPERTS = 16
D_EXPERT = 256
ROUTER_LANES = 128
ALPHA = (2 * DEPTH) ** 0.25
LN_EPS = 1e-5
GN_EPS = 64e-5
J_HALF = HEAD // 2
VMEM_LIMIT = 56 * 1024 * 1024


def _params(n_axes=1):
    return pltpu.CompilerParams(dimension_semantics=("arbitrary",) * n_axes,
                                vmem_limit_bytes=VMEM_LIMIT)


def _full(shape):
    return pl.BlockSpec(shape, lambda *_: (0,) * len(shape))


def _const(shape):
    return pl.BlockSpec(shape, lambda *_: (0,) * len(shape), pipeline_mode=pl.Buffered(1))


def _layer_norm(x):
    mu = jnp.mean(x, axis=-1, keepdims=True)
    xc = x - mu
    var = jnp.mean(xc * xc, axis=-1, keepdims=True)
    return xc * lax.rsqrt(var + LN_EPS)


def _softplus(z):
    return jnp.maximum(z, 0.0) + jnp.log1p(jnp.exp(-jnp.abs(z)))


def _sigmoid(z):
    return 1.0 / (1.0 + jnp.exp(-z))


def _gelu_tanh(x):
    c = 0.7978845608028654
    return x * (0.5 * (1.0 + jnp.tanh(c * (x + 0.044715 * (x * x * x)))))


def _dot(a, b):
    return jnp.dot(a, b, preferred_element_type=F32)


def _split(x, precise=True):
    hi = x.astype(BF16)
    return hi, ((x - hi.astype(F32)).astype(BF16) if precise else None)


def _mm(xs, w_hi, w_lo=None):
    hi, lo = xs
    out = _dot(hi, w_hi)
    if w_lo is not None:
        out = out + _dot(lo, w_hi) + _dot(hi, w_lo)
    return out


def _split_dot(x, w_bf16):
    hi, lo = _split(x)
    return _dot(hi, w_bf16) + _dot(lo, w_bf16)


def _load_time_major(x_ref, tm_ref, batch_major, tt):
    if not batch_major:
        return x_ref[...]
    for t in range(tt):
        tm_ref[t] = x_ref[:, t, :]
    return tm_ref[...]


def _store_batch_major(o_ref, val, tt):
    for t in range(tt):
        o_ref[:, t, :] = val[t]


def _ada_kernel(c_ref, w_ref, b_ref, o_ref):
    c = c_ref[...]
    w_hi, w_lo = _split(w_ref[0])
    o_ref[0] = _mm(_split(c * _sigmoid(c)), w_hi, w_lo) + b_ref[0]


def _ada_call(c_all, w_ada, b_ada):
    n = c_all.shape[0]
    tn = 1536
    return pl.pallas_call(
        _ada_kernel,
        grid=(DEPTH, 6 * D_MODEL // tn),
        in_specs=[pl.BlockSpec((n, D_MODEL), lambda l, j: (0, 0)),
                  pl.BlockSpec((1, D_MODEL, tn), lambda l, j: (l, 0, j)),
                  pl.BlockSpec((1, 1, tn), lambda l, j: (l, 0, j))],
        out_specs=pl.BlockSpec((1, n, tn), lambda l, j: (l, 0, j)),
        out_shape=jax.ShapeDtypeStruct((DEPTH, n, 6 * D_MODEL), F32),
        compiler_params=_params(2),
        name="ada_mod",
    )(c_all, w_ada, b_ada.reshape(DEPTH, 1, 6 * D_MODEL))


N_PRE_MATS = 5


def _mix_pre_kernel(precise, batch_major, tt, bsz,
                    x_ref, sh_ref, sc_ref, hprev_ref, cst_ref, lst_ref, *refs):
    n_mat = N_PRE_MATS * (2 if precise else 1)
    mats = refs[:n_mat]
    if precise:
        (win_ref, win_lo), (wdu_ref, wdu_lo), (wau_ref, wau_lo), (wgu_ref, wgu_lo), (wg_ref, wg_lo) = (
            (mats[2 * i], mats[2 * i + 1]) for i in range(N_PRE_MATS))
    else:
        (win_ref, win_lo), (wdu_ref, wdu_lo), (wau_ref, wau_lo), (wgu_ref, wgu_lo), (wg_ref, wg_lo) = (
            (m, None) for m in mats)
    (mu_ref, w0_ref, a0_ref, kk_ref, ka_ref, rk_ref, cw_ref, cb_ref, bg_ref, lam_ref, gones_ref,
     r_o, w_o, k_o, v_o, a_o, b_o, g_o, bonus_o, yb_o, shift_o, conv_o, lru_o,
     prevp_s, conv_s, lru_s, a_s, u_s, hs_s, *tm_s) = refs[n_mat:]
    rows = tt * bsz

    def win(lo_hi_ref, c0, c1):
        return None if lo_hi_ref is None else lo_hi_ref[:, c0:c1]

    @pl.when(pl.program_id(0) == 0)
    def _():
        prevp_s[...] = _mm(_split(hprev_ref[...], precise), win_ref[:, :N_SHIFT_PAD], win(win_lo, 0, N_SHIFT_PAD))
        conv_s[...] = cst_ref[...]
        lru_s[...] = lst_ref[...]

    x = _load_time_major(x_ref, tm_s[0] if batch_major else None, batch_major, tt)
    h = _layer_norm(x) * (1.0 + sc_ref[...]) + sh_ref[...]
    shift_o[...] = h[tt - 1]
    hb = _split(h.reshape(rows, D_MODEL), precise)

    def shifted(off, width):
        p = _mm(hb, win_ref[:, off:off + width], win(win_lo, off, off + width)).reshape(tt, bsz, width)
        first = prevp_s[:, off:off + width][None]
        prev = jnp.concatenate([first, p[:tt - 1]], axis=0) if tt > 1 else first
        prevp_s[:, off:off + width] = p[tt - 1]
        return p + (prev - p) * mu_ref[:, off:off + width]

    def up(z, w_ref_, w_lo_):
        zs = _split(z.reshape(rows, LORA_PAD), precise)
        return _mm(zs, w_ref_[...], None if w_lo_ is None else w_lo_[...]).reshape(tt, bsz, C_A)

    def gsum(z):
        return _split_dot(z.reshape(rows, C_A), gones_ref[...]).reshape(tt, bsz, C_A)

    r = shifted(0, C_A)
    k = shifted(C_A, C_A)
    v = shifted(2 * C_A, C_A)
    lora = shifted(3 * C_A, 3 * LORA_PAD)
    wd = lora[:, :, 0:LORA_PAD]
    ad = lora[:, :, LORA_PAD:2 * LORA_PAD]
    gd = lora[:, :, 2 * LORA_PAD:3 * LORA_PAD]

    w_log = -_softplus(-(w0_ref[...] + up(jnp.tanh(wd), wdu_ref, wdu_lo))) - 0.5
    decay = jnp.exp(-jnp.exp(w_log))
    a_sig = _sigmoid(a0_ref[...] + up(ad, wau_ref, wau_lo))
    g_o[...] = up(_sigmoid(gd), wgu_ref, wgu_lo)

    kk = k * kk_ref[...]
    kk = kk / jnp.maximum(jnp.sqrt(gsum(kk * kk)), 1e-12)
    k2 = k * (1.0 + (a_sig - 1.0) * ka_ref[...])
    for o_ref, val in ((r_o, r), (w_o, decay), (k_o, k2), (v_o, v), (a_o, -kk), (b_o, kk * a_sig)):
        _store_batch_major(o_ref, val, tt)
    bonus_o[...] = gsum(r * k2 * rk_ref[...]) * v

    g0, g1 = N_SHIFT_PAD, N_SHIFT_PAD + C_B
    gate_b = _mm(hb, win_ref[:, g0:g1], win(win_lo, g0, g1)).reshape(tt, bsz, C_B)
    rec_b = _mm(hb, win_ref[:, g1:N_IN_PAD], win(win_lo, g1, N_IN_PAD)).reshape(tt, bsz, C_B)
    xp = jnp.concatenate([conv_s[...], rec_b], axis=0)
    xc = cb_ref[...] + xp[0:tt] * cw_ref[0:1, :]
    for j in range(1, CONV_W):
        xc = xc + xp[j:j + tt] * cw_ref[j:j + 1, :]
    conv_new = xp[tt:tt + CONV_W - 1]
    conv_s[...] = conv_new
    conv_o[...] = conv_new

    gates = _mm(_split(xc.reshape(rows, C_B), precise), wg_ref[...],
                None if wg_lo is None else wg_lo[...]) + bg_ref[...]
    r_t = _sigmoid(gates[:, :C_B]).reshape(tt, bsz, C_B)
    i_t = _sigmoid(gates[:, C_B:]).reshape(tt, bsz, C_B)
    log_a = -LRU_C * r_t * _softplus(-lam_ref[...])
    a_s[...] = jnp.exp(log_a)
    u_s[...] = jnp.sqrt(1.0 - jnp.exp(2.0 * log_a)) * (i_t * xc)

    def scan_step(t, hcur):
        hcur = a_s[t] * hcur + u_s[t]
        hs_s[t] = hcur
        return hcur

    h_fin = lax.fori_loop(0, tt, scan_step, lru_s[...])
    lru_s[...] = h_fin
    lru_o[...] = h_fin
    yb_o[...] = hs_s[...] * _gelu_tanh(gate_b)


def _mix_pre_call(x, batch_major, tt, sh1, sc1, h_prev, conv_state, lru_state, wl):
    precise = wl['precise']
    if batch_major:
        bsz, t_len, _ = x.shape
        x_spec = pl.BlockSpec((bsz, tt, D_MODEL), lambda c: (0, c, 0))
    else:
        t_len, bsz, _ = x.shape
        x_spec = pl.BlockSpec((tt, bsz, D_MODEL), lambda c: (c, 0, 0))
    seq = lambda ch: pl.BlockSpec((tt, bsz, ch), lambda c: (c, 0, 0))
    weights = []
    for name in ('w_in', 'w_decay_up', 'w_a_up', 'w_g_up', 'w_gates'):
        weights.append(wl[name])
        if precise:
            weights.append(wl[name + '_lo'])
    weights += [wl['mu'], wl['w0'], wl['a0'], wl['k_k'], wl['k_a'], wl['r_k'], wl['conv_w'], wl['conv_b'],
                wl['b_gates'], wl['lam'], wl['gones']]
    seq_out = jax.ShapeDtypeStruct((t_len, bsz, C_A), F32)
    scan_spec = pl.BlockSpec((bsz, tt, C_A), lambda c: (0, c, 0))
    scan_out = jax.ShapeDtypeStruct((bsz, t_len, C_A), F32)
    scratch = [pltpu.VMEM((bsz, N_SHIFT_PAD), F32), pltpu.VMEM((CONV_W - 1, bsz, C_B), F32),
               pltpu.VMEM((bsz, C_B), F32), pltpu.VMEM((tt, bsz, C_B), F32),
               pltpu.VMEM((tt, bsz, C_B), F32), pltpu.VMEM((tt, bsz, C_B), F32)]
    if batch_major:
        scratch.append(pltpu.VMEM((tt, bsz, D_MODEL), F32))
    return pl.pallas_call(
        functools.partial(_mix_pre_kernel, precise, batch_major, tt, bsz),
        grid=(t_len // tt,),
        in_specs=[x_spec, _full((bsz, D_MODEL)), _full((bsz, D_MODEL)), _full((bsz, D_MODEL)),
                  _full((CONV_W - 1, bsz, C_B)), _full((bsz, C_B))] + [_const(w.shape) for w in weights],
        out_specs=[scan_spec] * 6 + [seq(C_A)] * 3 + [_full((bsz, D_MODEL)), _full((CONV_W - 1, bsz, C_B)),
                                                      _full((bsz, C_B))],
        out_shape=[scan_out] * 6 + [seq_out] * 3 + [jax.ShapeDtypeStruct((bsz, D_MODEL), F32),
                                   jax.ShapeDtypeStruct((CONV_W - 1, bsz, C_B), F32),
                                   jax.ShapeDtypeStruct((bsz, C_B), F32)],
        scratch_shapes=scratch,
        compiler_params=_params(),
        name="mix_pre",
    )(x, sh1, sc1, h_prev, conv_state, lru_state, *weights)


I_LO = HEAD // 2
SUB = 8
N_STRIP = I_LO // SUB


def _strips(ref, *lead):
    return [ref[(*lead, pl.ds(q * SUB, SUB), slice(None))] for q in range(N_STRIP)]


def _row(ref, *idx):
    return jnp.broadcast_to(ref[(*idx, slice(None))], (SUB, 2 * HEAD))


A_OP, W_OP, B_OP, K_OP, R_OP = range(5)
RELAYOUT_UNROLL = 8


V_RAW = 5


def _wkv_scan_kernel(tt, a_hbm, w_hbm, b_hbm, k_hbm, r_hbm, v_hbm, s0_ref,
                     y_ref, sfin_ref, s_s, raw_s, raw_sem, ops_s, zt_s, vop_s, vt_s, ysc_s):
    hbm = (a_hbm, w_hbm, b_hbm, k_hbm, r_hbm, v_hbm)
    bsz = raw_s.shape[1]
    inst = bsz * N_HEADS
    chunk = pl.program_id(0)

    def raw_copy(n, c):
        t0 = pl.multiple_of(c * tt, tt)
        return pltpu.make_async_copy(hbm[n].at[:, pl.ds(t0, tt), :], raw_s.at[n], raw_sem.at[n])

    @pl.when(chunk == 0)
    def _():
        for n in range(len(hbm)):
            raw_copy(n, 0).start()
        s_s[...] = s0_ref[...]

    for n in range(V_RAW):
        raw_copy(n, chunk).wait()
        for bi in range(bsz):
            zt_s[bi * N_HEADS:(bi + 1) * N_HEADS] = raw_s[n, bi].T.reshape(N_HEADS, HEAD, tt)

        def key_rows(j, carry, n=n):
            rows = zt_s[:, j, :]
            ops_s[n, j] = jnp.concatenate([rows, rows], axis=0).T
            return carry

        lax.fori_loop(0, HEAD, key_rows, 0, unroll=RELAYOUT_UNROLL)

    raw_copy(V_RAW, chunk).wait()
    for bi in range(bsz):
        vt_s[bi * N_HEADS:(bi + 1) * N_HEADS] = raw_s[V_RAW, bi].T.reshape(N_HEADS, 2, I_LO, tt)

    def value_rows(il, carry):
        vop_s[:, il, :] = jnp.concatenate([vt_s[:, 0, il, :], vt_s[:, 1, il, :]], axis=0).T
        return carry

    lax.fori_loop(0, I_LO, value_rows, 0, unroll=RELAYOUT_UNROLL)

    @pl.when(chunk + 1 < pl.num_programs(0))
    def _():
        for n in range(len(hbm)):
            raw_copy(n, chunk + 1).start()

    sa = [None] * N_STRIP
    for j in range(HEAD):
        a8 = _row(ops_s, A_OP, j, pl.ds(0, 1))
        for q, s in enumerate(_strips(s_s, j)):
            sa[q] = s * a8 if sa[q] is None else sa[q] + s * a8

    def step(t, sa):
        t_next = jnp.minimum(t + 1, tt - 1)
        v = _strips(vop_s, t)
        y = [None] * N_STRIP
        sa_next = [None] * N_STRIP
        for j in range(HEAD):
            w8, b8, k8, r8 = (_row(ops_s, n, j, pl.ds(t, 1)) for n in (W_OP, B_OP, K_OP, R_OP))
            a8 = _row(ops_s, A_OP, j, pl.ds(t_next, 1))
            for q in range(N_STRIP):
                rows = pl.ds(q * SUB, SUB)
                s_new = s_s[j, rows, :] * w8 + sa[q] * b8 + v[q] * k8
                s_s[j, rows, :] = s_new
                yq, aq = s_new * r8, s_new * a8
                y[q] = yq if y[q] is None else y[q] + yq
                sa_next[q] = aq if sa_next[q] is None else sa_next[q] + aq
        for q in range(N_STRIP):
            ysc_s[t, pl.ds(q * SUB, SUB), :] = y[q]
        return tuple(sa_next)

    lax.fori_loop(0, tt, step, tuple(sa))

    def y_rows(il, carry):
        yt = ysc_s[:, il, :].T
        vt_s[:, 0, il, :] = yt[:inst]
        vt_s[:, 1, il, :] = yt[inst:]
        return carry

    lax.fori_loop(0, I_LO, y_rows, 0, unroll=RELAYOUT_UNROLL)
    for bi in range(bsz):
        y_ref[bi] = vt_s[bi * N_HEADS:(bi + 1) * N_HEADS].reshape(C_A, tt).T

    @pl.when(pl.program_id(0) == pl.num_programs(0) - 1)
    def _():
        sfin_ref[...] = s_s[...]


def _wkv_scan_call(a, w, b, k, r, v, s0, tt):
    bsz, t_len, _ = w.shape
    inst = bsz * N_HEADS
    state = _const((HEAD, I_LO, 2 * HEAD))
    return pl.pallas_call(
        functools.partial(_wkv_scan_kernel, tt),
        grid=(t_len // tt,),
        in_specs=[pl.BlockSpec(memory_space=pl.ANY)] * 6 + [state],
        out_specs=[pl.BlockSpec((bsz, tt, C_A), lambda c: (0, c, 0)), _full((HEAD, I_LO, 2 * HEAD))],
        out_shape=[jax.ShapeDtypeStruct((bsz, t_len, C_A), F32),
                   jax.ShapeDtypeStruct((HEAD, I_LO, 2 * HEAD), F32)],
        scratch_shapes=[pltpu.VMEM((HEAD, I_LO, 2 * HEAD), F32),
                        pltpu.VMEM((6, bsz, tt, C_A), F32),
                        pltpu.SemaphoreType.DMA((6,)),
                        pltpu.VMEM((5, HEAD, tt, 2 * inst), F32),
                        pltpu.VMEM((inst, HEAD, tt), F32),
                        pltpu.VMEM((tt, I_LO, 2 * inst), F32),
                        pltpu.VMEM((inst, 2, I_LO, tt), F32),
                        pltpu.VMEM((tt, I_LO, 2 * inst), F32)],
        compiler_params=_params(),
        name="wkv_scan",
    )(a, w, b, k, r, v, s0)


def _wkv_step_kernel(a_ref, w_ref, b_ref, k_ref, r_ref, v_ref, s_ref, y_ref, so_ref):
    s = s_ref[...]
    row = lax.broadcasted_iota(jnp.int32, (HEAD, HEAD), 0)
    col = lax.broadcasted_iota(jnp.int32, (HEAD, HEAD), 1)
    eye = (row == col).astype(F32)
    sa = jnp.sum(s * a_ref[...], axis=-1, keepdims=True)
    v_col = jnp.sum(eye * v_ref[...], axis=-1, keepdims=True)
    s_new = s * w_ref[...] + sa * b_ref[...] + v_col * k_ref[...]
    so_ref[...] = s_new
    y_col = jnp.sum(s_new * r_ref[...], axis=-1, keepdims=True)
    y_ref[...] = jnp.sum(eye * y_col, axis=-2, keepdims=True)


def _wkv_step_call(a, w, b, k, r, v, s, bb):
    bsz = s.shape[0]
    op = pl.BlockSpec((bb, N_HEADS, 1, HEAD), lambda c: (c, 0, 0, 0))
    st = pl.BlockSpec((bb, N_HEADS, HEAD, HEAD), lambda c: (c, 0, 0, 0))
    return pl.pallas_call(
        _wkv_step_kernel,
        grid=(bsz // bb,),
        in_specs=[op] * 6 + [st],
        out_specs=[op, st],
        out_shape=[jax.ShapeDtypeStruct((bsz, N_HEADS, 1, HEAD), F32),
                   jax.ShapeDtypeStruct((bsz, N_HEADS, HEAD, HEAD), F32)],
        compiler_params=_params(),
        name="wkv_step",
    )(a, w, b, k, r, v, s)


def _route(logits):
    lane = lax.broadcasted_iota(jnp.int32, logits.shape, 1)
    neg = jnp.float32(-jnp.inf)
    big = jnp.int32(ROUTER_LANES)
    is_grp = (lane >= N_EXPERTS) & (lane < N_EXPERTS + N_GROUPS)
    gl = jnp.where(is_grp, logits, neg)
    gmax = jnp.max(gl, axis=-1, keepdims=True)
    ge = jnp.where(is_grp, jnp.exp(gl - gmax), 0.0)
    gp = ge / jnp.sum(ge, axis=-1, keepdims=True)
    gi = jnp.min(jnp.where(gl == gmax, lane, big), axis=-1, keepdims=True)
    p_grp = jnp.sum(jnp.where(lane == gi, gp, 0.0), axis=-1, keepdims=True)
    gidx = gi - N_EXPERTS
    in_grp = (lane >= gidx * EXP_PER_GROUP) & (lane < (gidx + 1) * EXP_PER_GROUP)
    el = jnp.where(in_grp, logits, neg)
    emax = jnp.max(el, axis=-1, keepdims=True)
    ee = jnp.where(in_grp, jnp.exp(el - emax), 0.0)
    pe = ee / jnp.sum(ee, axis=-1, keepdims=True)
    pe_m = jnp.where(in_grp, pe, -1.0)
    v1 = jnp.max(pe_m, axis=-1, keepdims=True)
    i1 = jnp.min(jnp.where(pe_m == v1, lane, big), axis=-1, keepdims=True)
    pe_m2 = jnp.where(lane == i1, -1.0, pe_m)
    v2 = jnp.max(pe_m2, axis=-1, keepdims=True)
    i2 = jnp.min(jnp.where(pe_m2 == v2, lane, big), axis=-1, keepdims=True)
    tot = v1 + v2
    return jnp.where(lane == i1, v1 / tot * p_grp, 0.0) + jnp.where(lane == i2, v2 / tot * p_grp, 0.0)


def _mix_post_kernel(precise, batch_major, tt, bsz,
                     ya_ref, bonus_ref, g_ref, yb_ref, x_ref, g1_ref, sh2_ref, sc2_ref, wout_ref, *refs):
    wout_lo = refs[0] if precise else None
    (lnxg_ref, lnxb_ref, gones_ref, ln1g_ref, ln1b_ref, wrh_ref, wrl_ref, br_ref,
     x1_o, h2_o, comb_o, ya_tm_s, *tm_s) = refs[1:] if precise else refs
    rows = tt * bsz

    def wout(lo_hi_ref, r0, r1):
        return None if lo_hi_ref is None else lo_hi_ref[r0:r1, :]

    def gsum(z):
        return _split_dot(z.reshape(rows, C_A), gones_ref[...]).reshape(tt, bsz, C_A)

    ya = _load_time_major(ya_ref, ya_tm_s, True, tt)
    mu = gsum(ya) * (1.0 / HEAD)
    yc = ya - mu
    var = gsum(yc * yc) * (1.0 / HEAD)
    yn = yc * lax.rsqrt(var + GN_EPS) * lnxg_ref[...] + lnxb_ref[...]
    ya = (yn + bonus_ref[...]) * g_ref[...]
    y = (_mm(_split(ya.reshape(rows, C_A), precise), wout_ref[0:C_A, :], wout(wout_lo, 0, C_A))
         + _mm(_split(yb_ref[...].reshape(rows, C_B), precise), wout_ref[C_A:, :],
               wout(wout_lo, C_A, C_A + C_B))).reshape(tt, bsz, D_MODEL)
    x = _load_time_major(x_ref, tm_s[0] if batch_major else None, batch_major, tt)
    x1 = _layer_norm(ALPHA * x + (1.0 + g1_ref[...]) * y) * ln1g_ref[...] + ln1b_ref[...]
    x1_o[...] = x1
    h2 = (_layer_norm(x1) * (1.0 + sc2_ref[...]) + sh2_ref[...]).reshape(rows, D_MODEL)
    hi = h2.astype(BF16)
    lo = (h2 - hi.astype(F32)).astype(BF16)
    h2_o[...] = h2 if h2_o.dtype == F32 else hi
    logits = _dot(hi, wrh_ref[...]) + _dot(lo, wrh_ref[...]) + _dot(hi, wrl_ref[...]) + br_ref[...]
    comb_o[...] = _route(logits)


def _mix_post_call(ya, bonus, g, yb, x, batch_major, tt, g1, sh2, sc2, wl):
    bsz, t_len, _ = ya.shape
    rows = tt * bsz
    ya_spec = pl.BlockSpec((bsz, tt, C_A), lambda c: (0, c, 0))
    if batch_major:
        x_spec = pl.BlockSpec((bsz, tt, D_MODEL), lambda c: (0, c, 0))
    else:
        x_spec = pl.BlockSpec((tt, bsz, D_MODEL), lambda c: (c, 0, 0))
    seq = pl.BlockSpec((tt, bsz, C_A), lambda c: (c, 0, 0))
    precise = wl['precise']
    weights = [wl['w_out']] + ([wl['w_out_lo']] if precise else []) + [
        wl['lnx_gain'], wl['lnx_bias'], wl['gones'], wl['ln1_gain'], wl['ln1_bias'],
        wl['w_router_hi'], wl['w_router_lo'], wl['b_router']]
    scratch = [pltpu.VMEM((tt, bsz, C_A), F32)] + ([pltpu.VMEM((tt, bsz, D_MODEL), F32)] if batch_major else [])
    return pl.pallas_call(
        functools.partial(_mix_post_kernel, precise, batch_major, tt, bsz),
        grid=(t_len // tt,),
        in_specs=[ya_spec] + [seq] * 3 + [x_spec] + [_full((bsz, D_MODEL))] * 3 + [_const(w.shape) for w in weights],
        out_specs=[pl.BlockSpec((tt, bsz, D_MODEL), lambda c: (c, 0, 0)),
                   pl.BlockSpec((rows, D_MODEL), lambda c: (c, 0)),
                   pl.BlockSpec((rows, ROUTER_LANES), lambda c: (c, 0))],
        out_shape=[jax.ShapeDtypeStruct((t_len, bsz, D_MODEL), F32),
                   jax.ShapeDtypeStruct((t_len * bsz, D_MODEL), F32 if wl['moe_precise'] else BF16),
                   jax.ShapeDtypeStruct((t_len * bsz, ROUTER_LANES), F32)],
        scratch_shapes=scratch,
        compiler_params=_params(),
        name="mix_post",
    )(ya, bonus, g, yb, x, g1, sh2, sc2, *weights)


def _moe_kernel(precise, batch_major_out, tt, bsz, h2_ref, comb_ref, x1_ref, g2_ref, *refs):
    if precise:
        wg_ref, wg_lo, wu_ref, wu_lo, wd_ref, wd_lo, ln2g_ref, ln2b_ref, o_ref, acc_s = refs
        lo = lambda r: r[0]
    else:
        wg_ref, wu_ref, wd_ref, ln2g_ref, ln2b_ref, o_ref, acc_s = refs
        wg_lo = wu_lo = wd_lo = None
        lo = lambda r: None
    e = pl.program_id(1)
    rows = tt * bsz

    @pl.when(e == 0)
    def _():
        acc_s[...] = jnp.zeros_like(acc_s)

    h = _split(h2_ref[...], True) if precise else (h2_ref[...], None)
    lane = lax.broadcasted_iota(jnp.int32, (rows, ROUTER_LANES), 1)
    ce = jnp.sum(jnp.where(lane == e, comb_ref[...], 0.0), axis=-1, keepdims=True)
    gate = _mm(h, wg_ref[0], lo(wg_lo))
    hid = (gate * _sigmoid(gate)) * _mm(h, wu_ref[0], lo(wu_lo))
    acc_s[...] += _mm(_split(hid * ce, precise), wd_ref[0], lo(wd_lo))

    @pl.when(e == N_EXPERTS - 1)
    def _():
        moe = acc_s[...].reshape(tt, bsz, D_MODEL)
        out = _layer_norm(ALPHA * x1_ref[...] + (1.0 + g2_ref[...]) * moe) * ln2g_ref[...] + ln2b_ref[...]
        if batch_major_out:
            _store_batch_major(o_ref, out, tt)
        else:
            o_ref[...] = out


def _moe_call(h2, comb, x1, batch_major_out, tt, g2, wl):
    precise = wl['moe_precise']
    t_len, bsz, _ = x1.shape
    rows = tt * bsz
    w_gate = pl.BlockSpec((1, D_MODEL, D_EXPERT), lambda c, e: (e, 0, 0))
    w_down = pl.BlockSpec((1, D_EXPERT, D_MODEL), lambda c, e: (e, 0, 0))
    weights, w_specs = [], []
    for name, spec in (('w_exp_gate', w_gate), ('w_exp_up', w_gate), ('w_exp_down', w_down)):
        weights.append(wl[name])
        w_specs.append(spec)
        if precise:
            weights.append(wl[name + '_lo'])
            w_specs.append(spec)
    if batch_major_out:
        o_spec = pl.BlockSpec((bsz, tt, D_MODEL), lambda c, e: (0, c, 0))
        o_shape = jax.ShapeDtypeStruct((bsz, t_len, D_MODEL), F32)
    else:
        o_spec = pl.BlockSpec((tt, bsz, D_MODEL), lambda c, e: (c, 0, 0))
        o_shape = jax.ShapeDtypeStruct((t_len, bsz, D_MODEL), F32)
    return pl.pallas_call(
        functools.partial(_moe_kernel, precise, batch_major_out, tt, bsz),
        grid=(t_len // tt, N_EXPERTS),
        in_specs=[pl.BlockSpec((rows, D_MODEL), lambda c, e: (c, 0)),
                  pl.BlockSpec((rows, ROUTER_LANES), lambda c, e: (c, 0)),
                  pl.BlockSpec((tt, bsz, D_MODEL), lambda c, e: (c, 0, 0)),
                  pl.BlockSpec((bsz, D_MODEL), lambda c, e: (0, 0))] + w_specs + [
                  pl.BlockSpec((1, D_MODEL), lambda c, e: (0, 0)),
                  pl.BlockSpec((1, D_MODEL), lambda c, e: (0, 0))],
        out_specs=o_spec,
        out_shape=o_shape,
        scratch_shapes=[pltpu.VMEM((rows, D_MODEL), F32)],
        compiler_params=_params(2),
        name="moe",
    )(h2, comb, x1, g2, *weights, wl['ln2_gain'], wl['ln2_bias'])


def _pad_cols(w, width):
    return jnp.pad(w, ((0, 0), (0, width - w.shape[1])))


def _pad_rows(w, height):
    return jnp.pad(w, ((0, height - w.shape[0]), (0, 0)))


def _block_diag(w):
    n, c, d = w.shape
    eye = jnp.eye(n, dtype=w.dtype)
    return (eye[:, None, :, None] * w[:, :, None, :]).reshape(n * c, n * d)


def _prep_layer(p, l, precise, moe_precise):
    row = lambda v: v[l].reshape(1, -1)
    w_in = p['w_in'][l]
    o1, o2, o3 = 3 * C_A, 3 * C_A + LORA_DECAY, 3 * C_A + LORA_DECAY + LORA_AAA
    pieces = [(0, o1, o1), (o1, o2, LORA_PAD), (o2, o3, LORA_PAD), (o3, N_SHIFT, LORA_PAD)]
    w_in_p = jnp.concatenate([_pad_cols(w_in[:, a:b], wd) for a, b, wd in pieces] + [w_in[:, N_SHIFT:]], axis=1)
    mu = p['mu_shift'][l].reshape(1, -1)
    mu_p = jnp.concatenate([_pad_cols(mu[:, a:b], wd) for a, b, wd in pieces], axis=1)
    w_router = jnp.concatenate([p['w_router_expert'][l], p['w_router_group'][l]], axis=1)
    w_router = _pad_cols(w_router, ROUTER_LANES)
    w_router_hi = w_router.astype(BF16)
    b_router = _pad_cols(jnp.concatenate([p['b_router_expert'][l], p['b_router_group'][l]]).reshape(1, -1),
                         ROUTER_LANES)
    head_id = jnp.arange(C_A) // HEAD
    mats = dict(
        w_in=w_in_p,
        w_decay_up=_pad_rows(p['w_decay_up'][l], LORA_PAD),
        w_a_up=_pad_rows(p['w_a_up'][l], LORA_PAD),
        w_g_up=_pad_rows(p['w_g_up'][l], LORA_PAD),
        w_gates=jnp.concatenate([_block_diag(p['w_rgate'][l]), _block_diag(p['w_igate'][l])], axis=1),
        w_out=p['w_out'][l])
    experts = dict(w_exp_gate=p['w_exp_gate'][l], w_exp_up=p['w_exp_up'][l], w_exp_down=p['w_exp_down'][l])
    split_mats = {}
    for group, with_lo in ((mats, precise), (experts, moe_precise)):
        for name, w in group.items():
            hi = w.astype(BF16)
            split_mats[name] = hi
            if with_lo:
                split_mats[name + '_lo'] = (w - hi.astype(F32)).astype(BF16)
    return dict(
        split_mats, precise=precise, moe_precise=moe_precise, mu=mu_p, w0=row(p['w0']), a0=row(p['a0']),
        k_k=row(p['k_k']), k_a=row(p['k_a']), r_k=row(p['r_k']),
        conv_w=p['conv_w'][l], conv_b=row(p['conv_b']),
        b_gates=jnp.concatenate([p['b_rgate'][l], p['b_igate'][l]]).reshape(1, -1),
        lam=row(p['lru_lambda']),
        gones=(head_id[:, None] == head_id[None, :]).astype(BF16),
        lnx_gain=row(p['lnx_gain']), lnx_bias=row(p['lnx_bias']),
        ln1_gain=row(p['ln1_gain']), ln1_bias=row(p['ln1_bias']),
        w_router_hi=w_router_hi, w_router_lo=(w_router - w_router_hi.astype(F32)).astype(BF16),
        b_router=b_router,
        ln2_gain=row(p['ln2_gain']), ln2_bias=row(p['ln2_bias']),
    )


def _trunk(x, batch_major, tt, mods, st_wkv, st_shift, st_conv, st_lru, layers, moe_tt, scan_tt):
    wkv_out, shift_out, conv_out, lru_out = [], [], [], []
    for l in range(DEPTH):
        wl = layers[l]
        sh1, sc1, g1, sh2, sc2, g2 = mods[l]
        bm_in = batch_major and l == 0
        conv_state = jnp.swapaxes(st_conv[l], 0, 1)
        (r, w, k, v, a, b, g, bonus, yb, shift_new, conv_new, lru_new) = _mix_pre_call(
            x, bm_in, tt, sh1, sc1, st_shift[l], conv_state, st_lru[l], wl)
        bsz, t_len, _ = r.shape
        if t_len > 1:
            s0 = st_wkv[l].reshape(bsz, N_HEADS, 2, I_LO, HEAD).transpose(4, 3, 2, 0, 1)
            s0 = s0.reshape(HEAD, I_LO, 2 * HEAD)
            ya, s_fin = _wkv_scan_call(a, w, b, k, r, v, s0, scan_tt)
            s_new = s_fin.reshape(HEAD, I_LO, 2, bsz, N_HEADS).transpose(3, 4, 2, 1, 0)
            s_new = s_new.reshape(bsz, N_HEADS, HEAD, HEAD)
        else:
            shp = (bsz, N_HEADS, 1, HEAD)
            y4, s_new = _wkv_step_call(a.reshape(shp), w.reshape(shp), b.reshape(shp), k.reshape(shp),
                                       r.reshape(shp), v.reshape(shp), st_wkv[l], 16)
            ya = y4.reshape(bsz, 1, C_A)
        x1, h2, comb = _mix_post_call(ya, bonus, g, yb, x, bm_in, tt, g1, sh2, sc2, wl)
        bm_out = batch_major and l == DEPTH - 1
        x = _moe_call(h2, comb, x1, bm_out, moe_tt, g2, wl)
        wkv_out.append(s_new)
        shift_out.append(shift_new)
        conv_out.append(jnp.swapaxes(conv_new, 0, 1))
        lru_out.append(lru_new)
    return x, jnp.stack(wkv_out), jnp.stack(shift_out), jnp.stack(conv_out), jnp.stack(lru_out)


def kernel(x_prompt, x_sample, c_prompt, c_sample, state_wkv, state_shift, state_conv, state_lru, w_ada, b_ada, w_in, mu_shift, w0, w_decay_up, a0, w_a_up, w_g_up, k_k, k_a, r_k, lnx_gain, lnx_bias, conv_w, conv_b, w_rgate, b_rgate, w_igate, b_igate, lru_lambda, w_out, ln1_gain, ln1_bias, w_router_group, b_router_group, w_router_expert, b_router_expert, w_exp_gate, w_exp_up, w_exp_down, ln2_gain, ln2_bias):
    p = dict(w_in=w_in, mu_shift=mu_shift, w0=w0, w_decay_up=w_decay_up, a0=a0, w_a_up=w_a_up,
             w_g_up=w_g_up, k_k=k_k, k_a=k_a, r_k=r_k.reshape(DEPTH, C_A), lnx_gain=lnx_gain,
             lnx_bias=lnx_bias, conv_w=conv_w, conv_b=conv_b, w_rgate=w_rgate, b_rgate=b_rgate,
             w_igate=w_igate, b_igate=b_igate, lru_lambda=lru_lambda, w_out=w_out, ln1_gain=ln1_gain,
             ln1_bias=ln1_bias, w_router_group=w_router_group, b_router_group=b_router_group,
             w_router_expert=w_router_expert, b_router_expert=b_router_expert, w_exp_gate=w_exp_gate,
             w_exp_up=w_exp_up, w_exp_down=w_exp_down, ln2_gain=ln2_gain, ln2_bias=ln2_bias)
    layers_p = [_prep_layer(p, l, precise=(l == 0), moe_precise=False) for l in range(DEPTH)]
    layers_s = [_prep_layer(p, l, precise=True, moe_precise=(l < DEPTH - 1)) for l in range(DEPTH)]
    bp, bs = x_prompt.shape[0], x_sample.shape[0]

    mod = _ada_call(jnp.concatenate([c_sample, c_prompt], axis=0), w_ada, b_ada)
    split = lambda m: [m[:, i * D_MODEL:(i + 1) * D_MODEL] for i in range(6)]
    mods_s = [split(mod[l, :bs]) for l in range(DEPTH)]
    mods_p = [split(mod[l, bs:bs + bp]) for l in range(DEPTH)]

    z_wkv = jnp.zeros((DEPTH, bp, N_HEADS, HEAD, HEAD), F32)
    z_shift = jnp.zeros((DEPTH, bp, D_MODEL), F32)
    z_conv = jnp.zeros((DEPTH, bp, CONV_W - 1, C_B), F32)
    z_lru = jnp.zeros((DEPTH, bp, C_B), F32)
    y_p, wkv_p, shift_p, conv_p, lru_p = _trunk(x_prompt, True, 64, mods_p, z_wkv, z_shift, z_conv, z_lru,
                                                layers_p, 128, 128)
    xs_tm = x_sample.reshape(1, bs, D_MODEL)
    y_s, wkv_s, shift_s, conv_s, lru_s = _trunk(xs_tm, False, 1, mods_s, state_wkv, state_shift, state_conv,
                                                state_lru, layers_s, 1, 1)
    return (y_p, y_s.reshape(bs, 1, D_MODEL), wkv_p, shift_p, conv_p, lru_p, wkv_s, shift_s, conv_s, lru_s)
```

```python
import functools

import jax
import jax.numpy as jnp
from jax import lax
from jax.experimental import pallas as pl
from jax.experimental.pallas import tpu as pltpu

F32 = jnp.float32
BF16 = jnp.bfloat16

D_MODEL = 1024
DEPTH = 2
C_A = 512
C_B = 512
HEAD = 64
N_HEADS = C_A // HEAD
LORA_DECAY = 32
LORA_AAA = 32
LORA_GATE = 96
LORA_PAD = 128
N_SHIFT = 3 * C_A + LORA_DECAY + LORA_AAA + LORA_GATE
N_SHIFT_PAD = 3 * C_A + 3 * LORA_PAD
N_IN_PAD = N_SHIFT_PAD + 2 * C_B
CONV_W = 4
LRU_BLOCKS = 8
LRU_C = 8.0
N_GROUPS = 4
EXP_PER_GROUP = 4
N_EXPERTS = 16
MOE_EXPERTS_PER_STEP = 4
D_EXPERT = 256
ROUTER_LANES = 128
ALPHA = (2 * DEPTH) ** 0.25
LN_EPS = 1e-5
GN_EPS = 64e-5
J_HALF = HEAD // 2
VMEM_LIMIT = 56 * 1024 * 1024


def _params(n_axes=1):
    return pltpu.CompilerParams(dimension_semantics=("arbitrary",) * n_axes,
                                vmem_limit_bytes=VMEM_LIMIT)


def _full(shape):
    return pl.BlockSpec(shape, lambda *_: (0,) * len(shape))


def _const(shape):
    return pl.BlockSpec(shape, lambda *_: (0,) * len(shape), pipeline_mode=pl.Buffered(1))


def _layer_norm(x):
    mu = jnp.mean(x, axis=-1, keepdims=True)
    xc = x - mu
    var = jnp.mean(xc * xc, axis=-1, keepdims=True)
    return xc * lax.rsqrt(var + LN_EPS)


def _softplus(z):
    return jnp.maximum(z, 0.0) + jnp.log1p(jnp.exp(-jnp.abs(z)))


def _sigmoid(z):
    return 1.0 / (1.0 + jnp.exp(-z))


def _gelu_tanh(x):
    c = 0.7978845608028654
    return x * (0.5 * (1.0 + jnp.tanh(c * (x + 0.044715 * (x * x * x)))))


def _dot(a, b):
    return jnp.dot(a, b, preferred_element_type=F32)


def _split(x, precise=True):
    hi = x.astype(BF16)
    return hi, ((x - hi.astype(F32)).astype(BF16) if precise else None)


def _mm(xs, w_hi, w_lo=None):
    hi, lo = xs
    out = _dot(hi, w_hi)
    if w_lo is not None:
        out = out + _dot(lo, w_hi) + _dot(hi, w_lo)
    return out


def _group_sum(x, ones_bf16, precise):
    hi, lo = _split(x, precise)
    out = _dot(hi, ones_bf16)
    return out + _dot(lo, ones_bf16) if precise else out


def _load_time_major(x_ref, tm_ref, batch_major, tt):
    if not batch_major:
        return x_ref[...]
    for t in range(tt):
        tm_ref[t] = x_ref[:, t, :]
    return tm_ref[...]


def _store_batch_major(o_ref, val, tt):
    for t in range(tt):
        o_ref[:, t, :] = val[t]


def _ada_kernel(c_ref, w_ref, b_ref, o_ref):
    c = c_ref[...]
    w_hi, w_lo = _split(w_ref[0])
    o_ref[0] = _mm(_split(c * _sigmoid(c)), w_hi, w_lo) + b_ref[0]


def _ada_call(c_all, w_ada, b_ada):
    n = c_all.shape[0]
    tn = 1536
    return pl.pallas_call(
        _ada_kernel,
        grid=(DEPTH, 6 * D_MODEL // tn),
        in_specs=[pl.BlockSpec((n, D_MODEL), lambda l, j: (0, 0)),
                  pl.BlockSpec((1, D_MODEL, tn), lambda l, j: (l, 0, j)),
                  pl.BlockSpec((1, 1, tn), lambda l, j: (l, 0, j))],
        out_specs=pl.BlockSpec((1, n, tn), lambda l, j: (l, 0, j)),
        out_shape=jax.ShapeDtypeStruct((DEPTH, n, 6 * D_MODEL), F32),
        compiler_params=_params(2),
        name="ada_mod",
    )(c_all, w_ada, b_ada.reshape(DEPTH, 1, 6 * D_MODEL))


MIX_PARTS = 2
N_PRE_MATS = 5


def _mix_pre_kernel(precise, batch_major, tt, bsz,
                    x_ref, sh_ref, sc_ref, hprev_ref, cst_ref, lst_ref, *refs):
    n_mat = N_PRE_MATS * (2 if precise else 1)
    mats = refs[:n_mat]
    if precise:
        (win_ref, win_lo), (wdu_ref, wdu_lo), (wau_ref, wau_lo), (wgu_ref, wgu_lo), (wg_ref, wg_lo) = (
            (mats[2 * i], mats[2 * i + 1]) for i in range(N_PRE_MATS))
    else:
        (win_ref, win_lo), (wdu_ref, wdu_lo), (wau_ref, wau_lo), (wgu_ref, wgu_lo), (wg_ref, wg_lo) = (
            (m, None) for m in mats)
    (mu_ref, w0_ref, a0_ref, kk_ref, ka_ref, rk_ref, cw_ref, cb_ref, bg_ref, lam_ref, gones_ref,
     r_o, w_o, k_o, v_o, a_o, b_o, g_o, bonus_o, yb_o, shift_o, conv_o, lru_o,
     prevp_s, conv_s, lru_s, a_s, u_s, hs_s, *tm_s) = refs[n_mat:]
    rows = tt * bsz

    def win(lo_hi_ref, c0, c1):
        return None if lo_hi_ref is None else lo_hi_ref[:, c0:c1]

    @pl.when(pl.program_id(0) == 0)
    def _():
        prevp_s[...] = _mm(_split(hprev_ref[...], precise), win_ref[:, :N_SHIFT_PAD], win(win_lo, 0, N_SHIFT_PAD))
        conv_s[...] = cst_ref[...]
        lru_s[...] = lst_ref[...]

    col_blocks = ((0, C_A), (C_A, C_A), (2 * C_A, C_A), (3 * C_A, 3 * LORA_PAD),
                  (N_SHIFT_PAD, C_B), (N_SHIFT_PAD + C_B, C_B))

    def project(t0, nt):
        nrows = nt * bsz
        if batch_major:
            for t in range(t0, t0 + nt):
                tm_s[0][t] = x_ref[:, t, :]
            x = tm_s[0][pl.ds(t0, nt)]
        else:
            x = x_ref[pl.ds(t0, nt)]
        h = _layer_norm(x) * (1.0 + sc_ref[...]) + sh_ref[...]
        if t0 + nt == tt:
            shift_o[...] = h[nt - 1]
        hb = _split(h.reshape(nrows, D_MODEL), precise)
        return [_mm(hb, win_ref[:, off:off + width], win(win_lo, off, off + width)).reshape(nt, bsz, width)
                for off, width in col_blocks]

    def finish(t0, nt, proj):
        nrows = nt * bsz
        steps = pl.ds(t0, nt)

        def shifted(idx):
            off, width = col_blocks[idx]
            p = proj[idx]
            first = prevp_s[:, off:off + width][None]
            prev = jnp.concatenate([first, p[:nt - 1]], axis=0) if nt > 1 else first
            prevp_s[:, off:off + width] = p[nt - 1]
            return p + (prev - p) * mu_ref[:, off:off + width]

        def up(z, w_ref_, w_lo_):
            zs = _split(z.reshape(nrows, LORA_PAD), precise)
            return _mm(zs, w_ref_[...], None if w_lo_ is None else w_lo_[...]).reshape(nt, bsz, C_A)

        def gsum(z):
            return _group_sum(z.reshape(nrows, C_A), gones_ref[...], precise).reshape(nt, bsz, C_A)

        r, k, v, lora = (shifted(idx) for idx in range(4))
        gate_b, rec_b = proj[4], proj[5]
        wd = lora[:, :, 0:LORA_PAD]
        ad = lora[:, :, LORA_PAD:2 * LORA_PAD]
        gd = lora[:, :, 2 * LORA_PAD:3 * LORA_PAD]

        w_log = -_softplus(-(w0_ref[...] + up(jnp.tanh(wd), wdu_ref, wdu_lo))) - 0.5
        decay = jnp.exp(-jnp.exp(w_log))
        a_sig = _sigmoid(a0_ref[...] + up(ad, wau_ref, wau_lo))
        g_o[steps] = up(_sigmoid(gd), wgu_ref, wgu_lo)

        kk = k * kk_ref[...]
        kk = kk / jnp.maximum(jnp.sqrt(gsum(kk * kk)), 1e-12)
        k2 = k * (1.0 + (a_sig - 1.0) * ka_ref[...])
        for o_ref, val in ((r_o, r), (w_o, decay), (k_o, k2), (v_o, v), (a_o, -kk), (b_o, kk * a_sig)):
            for t in range(nt):
                o_ref[:, t0 + t, :] = val[t]
        bonus_o[steps] = gsum(r * k2 * rk_ref[...]) * v

        xp = jnp.concatenate([conv_s[...], rec_b], axis=0)
        xc = cb_ref[...] + xp[0:nt] * cw_ref[0:1, :]
        for j in range(1, CONV_W):
            xc = xc + xp[j:j + nt] * cw_ref[j:j + 1, :]
        conv_s[...] = xp[nt:nt + CONV_W - 1]

        gates = _mm(_split(xc.reshape(nrows, C_B), precise), wg_ref[...],
                    None if wg_lo is None else wg_lo[...]) + bg_ref[...]
        r_t = _sigmoid(gates[:, :C_B]).reshape(nt, bsz, C_B)
        i_t = _sigmoid(gates[:, C_B:]).reshape(nt, bsz, C_B)
        log_a = -LRU_C * r_t * _softplus(-lam_ref[...])
        a_s[steps] = jnp.exp(log_a)
        u_s[steps] = jnp.sqrt(1.0 - jnp.exp(2.0 * log_a)) * (i_t * xc)
        yb_o[steps] = _gelu_tanh(gate_b)

    n_parts = MIX_PARTS if tt % MIX_PARTS == 0 else 1
    nt = tt // n_parts
    proj = project(0, nt)
    for i in range(n_parts):
        proj_next = project((i + 1) * nt, nt) if i + 1 < n_parts else None
        finish(i * nt, nt, proj)
        proj = proj_next
    conv_o[...] = conv_s[...]

    def scan_step(t, hcur):
        hcur = a_s[t] * hcur + u_s[t]
        hs_s[t] = hcur
        return hcur

    h_fin = lax.fori_loop(0, tt, scan_step, lru_s[...])
    lru_s[...] = h_fin
    lru_o[...] = h_fin
    yb_o[...] = hs_s[...] * yb_o[...]


def _mix_pre_call(x, batch_major, tt, sh1, sc1, h_prev, conv_state, lru_state, wl):
    precise = wl['precise']
    if batch_major:
        bsz, t_len, _ = x.shape
        x_spec = pl.BlockSpec((bsz, tt, D_MODEL), lambda c: (0, c, 0))
    else:
        t_len, bsz, _ = x.shape
        x_spec = pl.BlockSpec((tt, bsz, D_MODEL), lambda c: (c, 0, 0))
    seq = lambda ch: pl.BlockSpec((tt, bsz, ch), lambda c: (c, 0, 0))
    weights = []
    for name in ('w_in', 'w_decay_up', 'w_a_up', 'w_g_up', 'w_gates'):
        weights.append(wl[name])
        if precise:
            weights.append(wl[name + '_lo'])
    weights += [wl['mu'], wl['w0'], wl['a0'], wl['k_k'], wl['k_a'], wl['r_k'], wl['conv_w'], wl['conv_b'],
                wl['b_gates'], wl['lam'], wl['gones']]
    seq_out = jax.ShapeDtypeStruct((t_len, bsz, C_A), F32)
    scan_spec = pl.BlockSpec((bsz, tt, C_A), lambda c: (0, c, 0))
    scan_out = jax.ShapeDtypeStruct((bsz, t_len, C_A), F32)
    scratch = [pltpu.VMEM((bsz, N_SHIFT_PAD), F32), pltpu.VMEM((CONV_W - 1, bsz, C_B), F32),
               pltpu.VMEM((bsz, C_B), F32), pltpu.VMEM((tt, bsz, C_B), F32),
               pltpu.VMEM((tt, bsz, C_B), F32), pltpu.VMEM((tt, bsz, C_B), F32)]
    if batch_major:
        scratch.append(pltpu.VMEM((tt, bsz, D_MODEL), F32))
    return pl.pallas_call(
        functools.partial(_mix_pre_kernel, precise, batch_major, tt, bsz),
        grid=(t_len // tt,),
        in_specs=[x_spec, _full((bsz, D_MODEL)), _full((bsz, D_MODEL)), _full((bsz, D_MODEL)),
                  _full((CONV_W - 1, bsz, C_B)), _full((bsz, C_B))] + [_const(w.shape) for w in weights],
        out_specs=[scan_spec] * 6 + [seq(C_A)] * 3 + [_full((bsz, D_MODEL)), _full((CONV_W - 1, bsz, C_B)),
                                                      _full((bsz, C_B))],
        out_shape=[scan_out] * 6 + [seq_out] * 3 + [jax.ShapeDtypeStruct((bsz, D_MODEL), F32),
                                   jax.ShapeDtypeStruct((CONV_W - 1, bsz, C_B), F32),
                                   jax.ShapeDtypeStruct((bsz, C_B), F32)],
        scratch_shapes=scratch,
        compiler_params=_params(),
        name="mix_pre",
    )(x, sh1, sc1, h_prev, conv_state, lru_state, *weights)


I_LO = HEAD // 2
SUB = 8
N_STRIP = I_LO // SUB


def _strips(ref, *lead):
    return [ref[(*lead, pl.ds(q * SUB, SUB), slice(None))] for q in range(N_STRIP)]


def _row(ref, *idx):
    return jnp.broadcast_to(ref[(*idx, slice(None))], (SUB, 2 * HEAD))


A_OP, W_OP, B_OP, K_OP, R_OP = range(5)
RELAYOUT_UNROLL = 8


V_RAW = 5


def _wkv_scan_kernel(tt, a_hbm, w_hbm, b_hbm, k_hbm, r_hbm, v_hbm, s0_ref,
                     y_ref, sfin_ref, s_s, raw_s, raw_sem, ops_s, zt_s, vop_s, vt_s, ysc_s):
    hbm = (a_hbm, w_hbm, b_hbm, k_hbm, r_hbm, v_hbm)
    bsz = raw_s.shape[1]
    inst = bsz * N_HEADS
    chunk = pl.program_id(0)

    def raw_copy(n, c):
        t0 = pl.multiple_of(c * tt, tt)
        return pltpu.make_async_copy(hbm[n].at[:, pl.ds(t0, tt), :], raw_s.at[n], raw_sem.at[n])

    @pl.when(chunk == 0)
    def _():
        for n in range(len(hbm)):
            raw_copy(n, 0).start()
        s_s[...] = s0_ref[...]

    for n in range(V_RAW):
        raw_copy(n, chunk).wait()
        for bi in range(bsz):
            zt_s[bi * N_HEADS:(bi + 1) * N_HEADS] = raw_s[n, bi].T.reshape(N_HEADS, HEAD, tt)

        def key_rows(j, carry, n=n):
            rows = zt_s[:, j, :]
            ops_s[n, j] = jnp.concatenate([rows, rows], axis=0).T
            return carry

        lax.fori_loop(0, HEAD, key_rows, 0, unroll=RELAYOUT_UNROLL)

    raw_copy(V_RAW, chunk).wait()
    for bi in range(bsz):
        vt_s[bi * N_HEADS:(bi + 1) * N_HEADS] = raw_s[V_RAW, bi].T.reshape(N_HEADS, 2, I_LO, tt)

    def value_rows(il, carry):
        vop_s[:, il, :] = jnp.concatenate([vt_s[:, 0, il, :], vt_s[:, 1, il, :]], axis=0).T
        return carry

    lax.fori_loop(0, I_LO, value_rows, 0, unroll=RELAYOUT_UNROLL)

    @pl.when(chunk + 1 < pl.num_programs(0))
    def _():
        for n in range(len(hbm)):
            raw_copy(n, chunk + 1).start()

    sa = [None] * N_STRIP
    for j in range(HEAD):
        a8 = _row(ops_s, A_OP, j, pl.ds(0, 1))
        for q, s in enumerate(_strips(s_s, j)):
            sa[q] = s * a8 if sa[q] is None else sa[q] + s * a8

    def step(t, sa):
        t_next = jnp.minimum(t + 1, tt - 1)
        v = _strips(vop_s, t)
        y = [None] * N_STRIP
        sa_next = [None] * N_STRIP
        for j in range(HEAD):
            w8, b8, k8, r8 = (_row(ops_s, n, j, pl.ds(t, 1)) for n in (W_OP, B_OP, K_OP, R_OP))
            a8 = _row(ops_s, A_OP, j, pl.ds(t_next, 1))
            for q in range(N_STRIP):
                rows = pl.ds(q * SUB, SUB)
                s_new = s_s[j, rows, :] * w8 + sa[q] * b8 + v[q] * k8
                s_s[j, rows, :] = s_new
                yq, aq = s_new * r8, s_new * a8
                y[q] = yq if y[q] is None else y[q] + yq
                sa_next[q] = aq if sa_next[q] is None else sa_next[q] + aq
        for q in range(N_STRIP):
            ysc_s[t, pl.ds(q * SUB, SUB), :] = y[q]
        return tuple(sa_next)

    lax.fori_loop(0, tt, step, tuple(sa))

    def y_rows(il, carry):
        yt = ysc_s[:, il, :].T
        vt_s[:, 0, il, :] = yt[:inst]
        vt_s[:, 1, il, :] = yt[inst:]
        return carry

    lax.fori_loop(0, I_LO, y_rows, 0, unroll=RELAYOUT_UNROLL)
    for bi in range(bsz):
        y_ref[bi] = vt_s[bi * N_HEADS:(bi + 1) * N_HEADS].reshape(C_A, tt).T

    @pl.when(pl.program_id(0) == pl.num_programs(0) - 1)
    def _():
        sfin_ref[...] = s_s[...]


def _wkv_scan_call(a, w, b, k, r, v, s0, tt):
    bsz, t_len, _ = w.shape
    inst = bsz * N_HEADS
    state = _const((HEAD, I_LO, 2 * HEAD))
    return pl.pallas_call(
        functools.partial(_wkv_scan_kernel, tt),
        grid=(t_len // tt,),
        in_specs=[pl.BlockSpec(memory_space=pl.ANY)] * 6 + [state],
        out_specs=[pl.BlockSpec((bsz, tt, C_A), lambda c: (0, c, 0)), _full((HEAD, I_LO, 2 * HEAD))],
        out_shape=[jax.ShapeDtypeStruct((bsz, t_len, C_A), F32),
                   jax.ShapeDtypeStruct((HEAD, I_LO, 2 * HEAD), F32)],
        scratch_shapes=[pltpu.VMEM((HEAD, I_LO, 2 * HEAD), F32),
                        pltpu.VMEM((6, bsz, tt, C_A), F32),
                        pltpu.SemaphoreType.DMA((6,)),
                        pltpu.VMEM((5, HEAD, tt, 2 * inst), F32),
                        pltpu.VMEM((inst, HEAD, tt), F32),
                        pltpu.VMEM((tt, I_LO, 2 * inst), F32),
                        pltpu.VMEM((inst, 2, I_LO, tt), F32),
                        pltpu.VMEM((tt, I_LO, 2 * inst), F32)],
        compiler_params=_params(),
        name="wkv_scan",
    )(a, w, b, k, r, v, s0)


def _wkv_step_kernel(a_ref, w_ref, b_ref, k_ref, r_ref, v_ref, s_ref, y_ref, so_ref):
    s = s_ref[...]
    row = lax.broadcasted_iota(jnp.int32, (HEAD, HEAD), 0)
    col = lax.broadcasted_iota(jnp.int32, (HEAD, HEAD), 1)
    eye = (row == col).astype(F32)
    sa = jnp.sum(s * a_ref[...], axis=-1, keepdims=True)
    v_col = jnp.sum(eye * v_ref[...], axis=-1, keepdims=True)
    s_new = s * w_ref[...] + sa * b_ref[...] + v_col * k_ref[...]
    so_ref[...] = s_new
    y_col = jnp.sum(s_new * r_ref[...], axis=-1, keepdims=True)
    y_ref[...] = jnp.sum(eye * y_col, axis=-2, keepdims=True)


def _wkv_step_call(a, w, b, k, r, v, s, bb):
    bsz = s.shape[0]
    op = pl.BlockSpec((bb, N_HEADS, 1, HEAD), lambda c: (c, 0, 0, 0))
    st = pl.BlockSpec((bb, N_HEADS, HEAD, HEAD), lambda c: (c, 0, 0, 0))
    return pl.pallas_call(
        _wkv_step_kernel,
        grid=(bsz // bb,),
        in_specs=[op] * 6 + [st],
        out_specs=[op, st],
        out_shape=[jax.ShapeDtypeStruct((bsz, N_HEADS, 1, HEAD), F32),
                   jax.ShapeDtypeStruct((bsz, N_HEADS, HEAD, HEAD), F32)],
        compiler_params=_params(),
        name="wkv_step",
    )(a, w, b, k, r, v, s)


def _route(logits):
    lane = lax.broadcasted_iota(jnp.int32, logits.shape, 1)
    neg = jnp.float32(-jnp.inf)
    big = jnp.int32(ROUTER_LANES)
    is_grp = (lane >= N_EXPERTS) & (lane < N_EXPERTS + N_GROUPS)
    gl = jnp.where(is_grp, logits, neg)
    gmax = jnp.max(gl, axis=-1, keepdims=True)
    ge = jnp.where(is_grp, jnp.exp(gl - gmax), 0.0)
    gp = ge / jnp.sum(ge, axis=-1, keepdims=True)
    gi = jnp.min(jnp.where(gl == gmax, lane, big), axis=-1, keepdims=True)
    p_grp = jnp.sum(jnp.where(lane == gi, gp, 0.0), axis=-1, keepdims=True)
    gidx = gi - N_EXPERTS
    in_grp = (lane >= gidx * EXP_PER_GROUP) & (lane < (gidx + 1) * EXP_PER_GROUP)
    el = jnp.where(in_grp, logits, neg)
    emax = jnp.max(el, axis=-1, keepdims=True)
    ee = jnp.where(in_grp, jnp.exp(el - emax), 0.0)
    pe = ee / jnp.sum(ee, axis=-1, keepdims=True)
    pe_m = jnp.where(in_grp, pe, -1.0)
    v1 = jnp.max(pe_m, axis=-1, keepdims=True)
    i1 = jnp.min(jnp.where(pe_m == v1, lane, big), axis=-1, keepdims=True)
    pe_m2 = jnp.where(lane == i1, -1.0, pe_m)
    v2 = jnp.max(pe_m2, axis=-1, keepdims=True)
    i2 = jnp.min(jnp.where(pe_m2 == v2, lane, big), axis=-1, keepdims=True)
    tot = v1 + v2
    return jnp.where(lane == i1, v1 / tot * p_grp, 0.0) + jnp.where(lane == i2, v2 / tot * p_grp, 0.0)


def _mix_post_kernel(precise, batch_major, tt, bsz,
                     ya_ref, bonus_ref, g_ref, yb_ref, x_ref, g1_ref, sh2_ref, sc2_ref, wout_ref, *refs):
    wout_lo = refs[0] if precise else None
    (lnxg_ref, lnxb_ref, gones_ref, ln1g_ref, ln1b_ref, wrh_ref, wrl_ref, br_ref,
     x1_o, h2_o, comb_o, ya_tm_s, *tm_s) = refs[1:] if precise else refs
    rows = tt * bsz

    def wout(lo_hi_ref, r0, r1):
        return None if lo_hi_ref is None else lo_hi_ref[r0:r1, :]

    def gsum(z):
        return _group_sum(z.reshape(rows, C_A), gones_ref[...], precise).reshape(tt, bsz, C_A)

    ya = _load_time_major(ya_ref, ya_tm_s, True, tt)
    mu = gsum(ya) * (1.0 / HEAD)
    yc = ya - mu
    var = gsum(yc * yc) * (1.0 / HEAD)
    yn = yc * lax.rsqrt(var + GN_EPS) * lnxg_ref[...] + lnxb_ref[...]
    ya = (yn + bonus_ref[...]) * g_ref[...]
    y = (_mm(_split(ya.reshape(rows, C_A), precise), wout_ref[0:C_A, :], wout(wout_lo, 0, C_A))
         + _mm(_split(yb_ref[...].reshape(rows, C_B), precise), wout_ref[C_A:, :],
               wout(wout_lo, C_A, C_A + C_B))).reshape(tt, bsz, D_MODEL)
    x = _load_time_major(x_ref, tm_s[0] if batch_major else None, batch_major, tt)
    x1 = _layer_norm(ALPHA * x + (1.0 + g1_ref[...]) * y) * ln1g_ref[...] + ln1b_ref[...]
    x1_o[...] = x1
    h2 = (_layer_norm(x1) * (1.0 + sc2_ref[...]) + sh2_ref[...]).reshape(rows, D_MODEL)
    hi = h2.astype(BF16)
    lo = (h2 - hi.astype(F32)).astype(BF16)
    h2_o[...] = h2 if h2_o.dtype == F32 else hi
    logits = _dot(hi, wrh_ref[...]) + _dot(lo, wrh_ref[...]) + _dot(hi, wrl_ref[...]) + br_ref[...]
    comb_o[...] = _route(logits)


def _mix_post_call(ya, bonus, g, yb, x, batch_major, tt, g1, sh2, sc2, wl):
    bsz, t_len, _ = ya.shape
    rows = tt * bsz
    ya_spec = pl.BlockSpec((bsz, tt, C_A), lambda c: (0, c, 0))
    if batch_major:
        x_spec = pl.BlockSpec((bsz, tt, D_MODEL), lambda c: (0, c, 0))
    else:
        x_spec = pl.BlockSpec((tt, bsz, D_MODEL), lambda c: (c, 0, 0))
    seq = pl.BlockSpec((tt, bsz, C_A), lambda c: (c, 0, 0))
    precise = wl['precise']
    weights = [wl['w_out']] + ([wl['w_out_lo']] if precise else []) + [
        wl['lnx_gain'], wl['lnx_bias'], wl['gones'], wl['ln1_gain'], wl['ln1_bias'],
        wl['w_router_hi'], wl['w_router_lo'], wl['b_router']]
    scratch = [pltpu.VMEM((tt, bsz, C_A), F32)] + ([pltpu.VMEM((tt, bsz, D_MODEL), F32)] if batch_major else [])
    return pl.pallas_call(
        functools.partial(_mix_post_kernel, precise, batch_major, tt, bsz),
        grid=(t_len // tt,),
        in_specs=[ya_spec] + [seq] * 3 + [x_spec] + [_full((bsz, D_MODEL))] * 3 + [_const(w.shape) for w in weights],
        out_specs=[pl.BlockSpec((tt, bsz, D_MODEL), lambda c: (c, 0, 0)),
                   pl.BlockSpec((rows, D_MODEL), lambda c: (c, 0)),
                   pl.BlockSpec((rows, ROUTER_LANES), lambda c: (c, 0))],
        out_shape=[jax.ShapeDtypeStruct((t_len, bsz, D_MODEL), F32),
                   jax.ShapeDtypeStruct((t_len * bsz, D_MODEL), F32 if wl['moe_precise'] else BF16),
                   jax.ShapeDtypeStruct((t_len * bsz, ROUTER_LANES), F32)],
        scratch_shapes=scratch,
        compiler_params=_params(),
        name="mix_post",
    )(ya, bonus, g, yb, x, g1, sh2, sc2, *weights)


def _moe_kernel(precise, batch_major_out, tt, bsz, h2_ref, comb_ref, x1_ref, g2_ref, *refs):
    if precise:
        wg_ref, wg_lo, wu_ref, wu_lo, wd_ref, wd_lo, ln2g_ref, ln2b_ref, o_ref, acc_s = refs
        lo = lambda r, i: r[i]
    else:
        wg_ref, wu_ref, wd_ref, ln2g_ref, ln2b_ref, o_ref, acc_s = refs
        wg_lo = wu_lo = wd_lo = None
        lo = lambda r, i: None
    step = pl.program_id(1)
    rows = tt * bsz

    @pl.when(step == 0)
    def _():
        acc_s[...] = jnp.zeros_like(acc_s)

    h = _split(h2_ref[...], True) if precise else (h2_ref[...], None)
    lane = lax.broadcasted_iota(jnp.int32, (rows, ROUTER_LANES), 1)
    comb = comb_ref[...]
    down = None
    for i in range(MOE_EXPERTS_PER_STEP):
        ce = jnp.sum(jnp.where(lane == step * MOE_EXPERTS_PER_STEP + i, comb, 0.0), axis=-1, keepdims=True)
        gate = _mm(h, wg_ref[i], lo(wg_lo, i))
        hid = (gate * _sigmoid(gate)) * _mm(h, wu_ref[i], lo(wu_lo, i))
        part = _mm(_split(hid * ce, precise), wd_ref[i], lo(wd_lo, i))
        down = part if down is None else down + part
    acc_s[...] += down

    @pl.when(step == N_EXPERTS // MOE_EXPERTS_PER_STEP - 1)
    def _():
        moe = acc_s[...].reshape(tt, bsz, D_MODEL)
        out = _layer_norm(ALPHA * x1_ref[...] + (1.0 + g2_ref[...]) * moe) * ln2g_ref[...] + ln2b_ref[...]
        if batch_major_out:
            _store_batch_major(o_ref, out, tt)
        else:
            o_ref[...] = out


def _moe_call(h2, comb, x1, batch_major_out, tt, g2, wl):
    precise = wl['moe_precise']
    t_len, bsz, _ = x1.shape
    rows = tt * bsz
    w_gate = pl.BlockSpec((MOE_EXPERTS_PER_STEP, D_MODEL, D_EXPERT), lambda c, e: (e, 0, 0))
    w_down = pl.BlockSpec((MOE_EXPERTS_PER_STEP, D_EXPERT, D_MODEL), lambda c, e: (e, 0, 0))
    weights, w_specs = [], []
    for name, spec in (('w_exp_gate', w_gate), ('w_exp_up', w_gate), ('w_exp_down', w_down)):
        weights.append(wl[name])
        w_specs.append(spec)
        if precise:
            weights.append(wl[name + '_lo'])
            w_specs.append(spec)
    if batch_major_out:
        o_spec = pl.BlockSpec((bsz, tt, D_MODEL), lambda c, e: (0, c, 0))
        o_shape = jax.ShapeDtypeStruct((bsz, t_len, D_MODEL), F32)
    else:
        o_spec = pl.BlockSpec((tt, bsz, D_MODEL), lambda c, e: (c, 0, 0))
        o_shape = jax.ShapeDtypeStruct((t_len, bsz, D_MODEL), F32)
    return pl.pallas_call(
        functools.partial(_moe_kernel, precise, batch_major_out, tt, bsz),
        grid=(t_len // tt, N_EXPERTS // MOE_EXPERTS_PER_STEP),
        in_specs=[pl.BlockSpec((rows, D_MODEL), lambda c, e: (c, 0)),
                  pl.BlockSpec((rows, ROUTER_LANES), lambda c, e: (c, 0)),
                  pl.BlockSpec((tt, bsz, D_MODEL), lambda c, e: (c, 0, 0)),
                  pl.BlockSpec((bsz, D_MODEL), lambda c, e: (0, 0))] + w_specs + [
                  pl.BlockSpec((1, D_MODEL), lambda c, e: (0, 0)),
                  pl.BlockSpec((1, D_MODEL), lambda c, e: (0, 0))],
        out_specs=o_spec,
        out_shape=o_shape,
        scratch_shapes=[pltpu.VMEM((rows, D_MODEL), F32)],
        compiler_params=_params(2),
        name="moe",
    )(h2, comb, x1, g2, *weights, wl['ln2_gain'], wl['ln2_bias'])


def _pad_cols(w, width):
    return jnp.pad(w, ((0, 0), (0, width - w.shape[1])))


def _pad_rows(w, height):
    return jnp.pad(w, ((0, height - w.shape[0]), (0, 0)))


def _block_diag(w):
    n, c, d = w.shape
    eye = jnp.eye(n, dtype=w.dtype)
    return (eye[:, None, :, None] * w[:, :, None, :]).reshape(n * c, n * d)


def _prep_layer(p, l, precise, moe_precise):
    row = lambda v: v[l].reshape(1, -1)
    w_in = p['w_in'][l]
    o1, o2, o3 = 3 * C_A, 3 * C_A + LORA_DECAY, 3 * C_A + LORA_DECAY + LORA_AAA
    pieces = [(0, o1, o1), (o1, o2, LORA_PAD), (o2, o3, LORA_PAD), (o3, N_SHIFT, LORA_PAD)]
    w_in_p = jnp.concatenate([_pad_cols(w_in[:, a:b], wd) for a, b, wd in pieces] + [w_in[:, N_SHIFT:]], axis=1)
    mu = p['mu_shift'][l].reshape(1, -1)
    mu_p = jnp.concatenate([_pad_cols(mu[:, a:b], wd) for a, b, wd in pieces], axis=1)
    w_router = jnp.concatenate([p['w_router_expert'][l], p['w_router_group'][l]], axis=1)
    w_router = _pad_cols(w_router, ROUTER_LANES)
    w_router_hi = w_router.astype(BF16)
    b_router = _pad_cols(jnp.concatenate([p['b_router_expert'][l], p['b_router_group'][l]]).reshape(1, -1),
                         ROUTER_LANES)
    head_id = jnp.arange(C_A) // HEAD
    mats = dict(
        w_in=w_in_p,
        w_decay_up=_pad_rows(p['w_decay_up'][l], LORA_PAD),
        w_a_up=_pad_rows(p['w_a_up'][l], LORA_PAD),
        w_g_up=_pad_rows(p['w_g_up'][l], LORA_PAD),
        w_gates=jnp.concatenate([_block_diag(p['w_rgate'][l]), _block_diag(p['w_igate'][l])], axis=1),
        w_out=p['w_out'][l])
    experts = dict(w_exp_gate=p['w_exp_gate'][l], w_exp_up=p['w_exp_up'][l], w_exp_down=p['w_exp_down'][l])
    split_mats = {}
    for group, with_lo in ((mats, precise), (experts, moe_precise)):
        for name, w in group.items():
            hi = w.astype(BF16)
            split_mats[name] = hi
            if with_lo:
                split_mats[name + '_lo'] = (w - hi.astype(F32)).astype(BF16)
    return dict(
        split_mats, precise=precise, moe_precise=moe_precise, mu=mu_p, w0=row(p['w0']), a0=row(p['a0']),
        k_k=row(p['k_k']), k_a=row(p['k_a']), r_k=row(p['r_k']),
        conv_w=p['conv_w'][l], conv_b=row(p['conv_b']),
        b_gates=jnp.concatenate([p['b_rgate'][l], p['b_igate'][l]]).reshape(1, -1),
        lam=row(p['lru_lambda']),
        gones=(head_id[:, None] == head_id[None, :]).astype(BF16),
        lnx_gain=row(p['lnx_gain']), lnx_bias=row(p['lnx_bias']),
        ln1_gain=row(p['ln1_gain']), ln1_bias=row(p['ln1_bias']),
        w_router_hi=w_router_hi, w_router_lo=(w_router - w_router_hi.astype(F32)).astype(BF16),
        b_router=b_router,
        ln2_gain=row(p['ln2_gain']), ln2_bias=row(p['ln2_bias']),
    )


def _trunk(x, batch_major, tt, mods, st_wkv, st_shift, st_conv, st_lru, layers, moe_tt, scan_tt):
    wkv_out, shift_out, conv_out, lru_out = [], [], [], []
    for l in range(DEPTH):
        wl = layers[l]
        sh1, sc1, g1, sh2, sc2, g2 = mods[l]
        bm_in = batch_major and l == 0
        conv_state = jnp.swapaxes(st_conv[l], 0, 1)
        (r, w, k, v, a, b, g, bonus, yb, shift_new, conv_new, lru_new) = _mix_pre_call(
            x, bm_in, tt, sh1, sc1, st_shift[l], conv_state, st_lru[l], wl)
        bsz, t_len, _ = r.shape
        if t_len > 1:
            s0 = st_wkv[l].reshape(bsz, N_HEADS, 2, I_LO, HEAD).transpose(4, 3, 2, 0, 1)
            s0 = s0.reshape(HEAD, I_LO, 2 * HEAD)
            ya, s_fin = _wkv_scan_call(a, w, b, k, r, v, s0, scan_tt)
            s_new = s_fin.reshape(HEAD, I_LO, 2, bsz, N_HEADS).transpose(3, 4, 2, 1, 0)
            s_new = s_new.reshape(bsz, N_HEADS, HEAD, HEAD)
        else:
            shp = (bsz, N_HEADS, 1, HEAD)
            y4, s_new = _wkv_step_call(a.reshape(shp), w.reshape(shp), b.reshape(shp), k.reshape(shp),
                                       r.reshape(shp), v.reshape(shp), st_wkv[l], 16)
            ya = y4.reshape(bsz, 1, C_A)
        x1, h2, comb = _mix_post_call(ya, bonus, g, yb, x, bm_in, tt, g1, sh2, sc2, wl)
        bm_out = batch_major and l == DEPTH - 1
        x = _moe_call(h2, comb, x1, bm_out, moe_tt, g2, wl)
        wkv_out.append(s_new)
        shift_out.append(shift_new)
        conv_out.append(jnp.swapaxes(conv_new, 0, 1))
        lru_out.append(lru_new)
    return x, jnp.stack(wkv_out), jnp.stack(shift_out), jnp.stack(conv_out), jnp.stack(lru_out)


def kernel(x_prompt, x_sample, c_prompt, c_sample, state_wkv, state_shift, state_conv, state_lru, w_ada, b_ada, w_in, mu_shift, w0, w_decay_up, a0, w_a_up, w_g_up, k_k, k_a, r_k, lnx_gain, lnx_bias, conv_w, conv_b, w_rgate, b_rgate, w_igate, b_igate, lru_lambda, w_out, ln1_gain, ln1_bias, w_router_group, b_router_group, w_router_expert, b_router_expert, w_exp_gate, w_exp_up, w_exp_down, ln2_gain, ln2_bias):
    p = dict(w_in=w_in, mu_shift=mu_shift, w0=w0, w_decay_up=w_decay_up, a0=a0, w_a_up=w_a_up,
             w_g_up=w_g_up, k_k=k_k, k_a=k_a, r_k=r_k.reshape(DEPTH, C_A), lnx_gain=lnx_gain,
             lnx_bias=lnx_bias, conv_w=conv_w, conv_b=conv_b, w_rgate=w_rgate, b_rgate=b_rgate,
             w_igate=w_igate, b_igate=b_igate, lru_lambda=lru_lambda, w_out=w_out, ln1_gain=ln1_gain,
             ln1_bias=ln1_bias, w_router_group=w_router_group, b_router_group=b_router_group,
             w_router_expert=w_router_expert, b_router_expert=b_router_expert, w_exp_gate=w_exp_gate,
             w_exp_up=w_exp_up, w_exp_down=w_exp_down, ln2_gain=ln2_gain, ln2_bias=ln2_bias)
    layers_p = [_prep_layer(p, l, precise=(l == 0), moe_precise=False) for l in range(DEPTH)]
    layers_s = [_prep_layer(p, l, precise=True, moe_precise=(l < DEPTH - 1)) for l in range(DEPTH)]
    bp, bs = x_prompt.shape[0], x_sample.shape[0]

    mod = _ada_call(jnp.concatenate([c_sample, c_prompt], axis=0), w_ada, b_ada)
    split = lambda m: [m[:, i * D_MODEL:(i + 1) * D_MODEL] for i in range(6)]
    mods_s = [split(mod[l, :bs]) for l in range(DEPTH)]
    mods_p = [split(mod[l, bs:bs + bp]) for l in range(DEPTH)]

    z_wkv = jnp.zeros((DEPTH, bp, N_HEADS, HEAD, HEAD), F32)
    z_shift = jnp.zeros((DEPTH, bp, D_MODEL), F32)
    z_conv = jnp.zeros((DEPTH, bp, CONV_W - 1, C_B), F32)
    z_lru = jnp.zeros((DEPTH, bp, C_B), F32)
    y_p, wkv_p, shift_p, conv_p, lru_p = _trunk(x_prompt, True, 64, mods_p, z_wkv, z_shift, z_conv, z_lru,
                                                layers_p, 128, 128)
    xs_tm = x_sample.reshape(1, bs, D_MODEL)
    y_s, wkv_s, shift_s, conv_s, lru_s = _trunk(xs_tm, False, 1, mods_s, state_wkv, state_shift, state_conv,
                                                state_lru, layers_s, 1, 1)
    return (y_p, y_s.reshape(bs, 1, D_MODEL), wkv_p, shift_p, conv_p, lru_p, wkv_s, shift_s, conv_s, lru_s)
```

```python
import functools

import jax
import jax.numpy as jnp
from jax import lax
from jax.experimental import pallas as pl
from jax.experimental.pallas import tpu as pltpu

F32 = jnp.float32
BF16 = jnp.bfloat16

D_MODEL = 1024
DEPTH = 2
C_A = 512
C_B = 512
HEAD = 64
N_HEADS = C_A // HEAD
LORA_DECAY = 32
LORA_AAA = 32
LORA_GATE = 96
LORA_PAD = 128
N_SHIFT = 3 * C_A + LORA_DECAY + LORA_AAA + LORA_GATE
N_SHIFT_PAD = 3 * C_A + 3 * LORA_PAD
N_IN_PAD = N_SHIFT_PAD + 2 * C_B
CONV_W = 4
LRU_BLOCKS = 8
LRU_C = 8.0
N_GROUPS = 4
EXP_PER_GROUP = 4
N_EXPERTS = 16
MOE_EXPERTS_PER_STEP = 4
D_EXPERT = 256
ROUTER_LANES = 128
ALPHA = (2 * DEPTH) ** 0.25
LN_EPS = 1e-5
GN_EPS = 64e-5
VMEM_LIMIT = 56 * 1024 * 1024
MIX_TT = 64
MOE_TT = 128
SCAN_TT = 128
WKV_STEP_BATCH = 16


def _params(n_axes=1):
    return pltpu.CompilerParams(dimension_semantics=("arbitrary",) * n_axes,
                                vmem_limit_bytes=VMEM_LIMIT)


def _full(shape):
    return pl.BlockSpec(shape, lambda *_: (0,) * len(shape))


def _const(shape):
    return pl.BlockSpec(shape, lambda *_: (0,) * len(shape), pipeline_mode=pl.Buffered(1))


def _layer_norm(x):
    mu = jnp.mean(x, axis=-1, keepdims=True)
    xc = x - mu
    var = jnp.mean(xc * xc, axis=-1, keepdims=True)
    return xc * lax.rsqrt(var + LN_EPS)


def _softplus(z):
    return jnp.maximum(z, 0.0) + jnp.log1p(jnp.exp(-jnp.abs(z)))


def _sigmoid(z):
    return 1.0 / (1.0 + jnp.exp(-z))


def _gelu_tanh(x):
    c = 0.7978845608028654
    return x * (0.5 * (1.0 + jnp.tanh(c * (x + 0.044715 * (x * x * x)))))


def _dot(a, b):
    return jnp.dot(a, b, preferred_element_type=F32)


def _split(x, precise=True):
    hi = x.astype(BF16)
    return hi, ((x - hi.astype(F32)).astype(BF16) if precise else None)


def _mm(xs, w_hi, w_lo=None):
    hi, lo = xs
    out = _dot(hi, w_hi)
    if w_lo is not None:
        out = out + _dot(lo, w_hi) + _dot(hi, w_lo)
    return out


def _group_sum(x, ones_bf16, precise):
    hi, lo = _split(x, precise)
    out = _dot(hi, ones_bf16)
    return out + _dot(lo, ones_bf16) if precise else out


def _load_batch_major(x_ref, tm_ref, t0, nt):
    for t in range(t0, t0 + nt):
        tm_ref[t] = x_ref[:, t, :]
    return tm_ref[pl.ds(t0, nt)]


def _store_batch_major(o_ref, val, t0, nt):
    for t in range(nt):
        o_ref[:, t0 + t, :] = val[t]


def _ada_kernel(c_ref, w_ref, b_ref, o_ref):
    c = c_ref[...]
    w_hi, w_lo = _split(w_ref[0])
    o_ref[0] = _mm(_split(c * _sigmoid(c)), w_hi, w_lo) + b_ref[0]


def _ada_call(c_all, w_ada, b_ada):
    n = c_all.shape[0]
    tn = 1536
    return pl.pallas_call(
        _ada_kernel,
        grid=(DEPTH, 6 * D_MODEL // tn),
        in_specs=[pl.BlockSpec((n, D_MODEL), lambda l, j: (0, 0)),
                  pl.BlockSpec((1, D_MODEL, tn), lambda l, j: (l, 0, j)),
                  pl.BlockSpec((1, 1, tn), lambda l, j: (l, 0, j))],
        out_specs=pl.BlockSpec((1, n, tn), lambda l, j: (l, 0, j)),
        out_shape=jax.ShapeDtypeStruct((DEPTH, n, 6 * D_MODEL), F32),
        compiler_params=_params(2),
        name="ada_mod",
    )(c_all, w_ada, b_ada.reshape(DEPTH, 1, 6 * D_MODEL))


MIX_PARTS = 2
N_PRE_MATS = 5


def _mix_pre_kernel(precise, batch_major, tt, bsz,
                    x_ref, sh_ref, sc_ref, hprev_ref, cst_ref, lst_ref, *refs):
    n_mat = N_PRE_MATS * (2 if precise else 1)
    mats = refs[:n_mat]
    if precise:
        (win_ref, win_lo), (wdu_ref, wdu_lo), (wau_ref, wau_lo), (wgu_ref, wgu_lo), (wg_ref, wg_lo) = (
            (mats[2 * i], mats[2 * i + 1]) for i in range(N_PRE_MATS))
    else:
        (win_ref, win_lo), (wdu_ref, wdu_lo), (wau_ref, wau_lo), (wgu_ref, wgu_lo), (wg_ref, wg_lo) = (
            (m, None) for m in mats)
    (mu_ref, w0_ref, a0_ref, kk_ref, ka_ref, rk_ref, cw_ref, cb_ref, bg_ref, lam_ref, gones_ref,
     r_o, w_o, k_o, v_o, a_o, b_o, g_o, bonus_o, yb_o, shift_o, conv_o, lru_o,
     prevp_s, conv_s, lru_s, a_s, u_s, hs_s, *tm_s) = refs[n_mat:]

    def win(lo_hi_ref, c0, c1):
        return None if lo_hi_ref is None else lo_hi_ref[:, c0:c1]

    @pl.when(pl.program_id(0) == 0)
    def _():
        prevp_s[...] = _mm(_split(hprev_ref[...], precise), win_ref[:, :N_SHIFT_PAD], win(win_lo, 0, N_SHIFT_PAD))
        conv_s[...] = cst_ref[...]
        lru_s[...] = lst_ref[...]

    col_blocks = ((0, C_A), (C_A, C_A), (2 * C_A, C_A), (3 * C_A, 3 * LORA_PAD),
                  (N_SHIFT_PAD, C_B), (N_SHIFT_PAD + C_B, C_B))

    def project(t0, nt):
        nrows = nt * bsz
        x = _load_batch_major(x_ref, tm_s[0], t0, nt) if batch_major else x_ref[pl.ds(t0, nt)]
        h = _layer_norm(x) * (1.0 + sc_ref[...]) + sh_ref[...]
        if t0 + nt == tt:
            shift_o[...] = h[nt - 1]
        hb = _split(h.reshape(nrows, D_MODEL), precise)
        return [_mm(hb, win_ref[:, off:off + width], win(win_lo, off, off + width)).reshape(nt, bsz, width)
                for off, width in col_blocks]

    def finish(t0, nt, proj):
        nrows = nt * bsz
        steps = pl.ds(t0, nt)

        def shifted(idx):
            off, width = col_blocks[idx]
            p = proj[idx]
            first = prevp_s[:, off:off + width][None]
            prev = jnp.concatenate([first, p[:nt - 1]], axis=0) if nt > 1 else first
            prevp_s[:, off:off + width] = p[nt - 1]
            return p + (prev - p) * mu_ref[:, off:off + width]

        def up(z, w_ref_, w_lo_):
            zs = _split(z.reshape(nrows, LORA_PAD), precise)
            return _mm(zs, w_ref_[...], None if w_lo_ is None else w_lo_[...]).reshape(nt, bsz, C_A)

        def gsum(z):
            return _group_sum(z.reshape(nrows, C_A), gones_ref[...], precise).reshape(nt, bsz, C_A)

        r, k, v, lora = (shifted(idx) for idx in range(4))
        gate_b, rec_b = proj[4], proj[5]
        wd = lora[:, :, 0:LORA_PAD]
        ad = lora[:, :, LORA_PAD:2 * LORA_PAD]
        gd = lora[:, :, 2 * LORA_PAD:3 * LORA_PAD]

        w_log = -_softplus(-(w0_ref[...] + up(jnp.tanh(wd), wdu_ref, wdu_lo))) - 0.5
        decay = jnp.exp(-jnp.exp(w_log))
        a_sig = _sigmoid(a0_ref[...] + up(ad, wau_ref, wau_lo))
        g_o[steps] = up(_sigmoid(gd), wgu_ref, wgu_lo)

        kk = k * kk_ref[...]
        kk = kk / jnp.maximum(jnp.sqrt(gsum(kk * kk)), 1e-12)
        k2 = k * (1.0 + (a_sig - 1.0) * ka_ref[...])
        for o_ref, val in ((r_o, r), (w_o, decay), (k_o, k2), (v_o, v), (a_o, -kk), (b_o, kk * a_sig)):
            _store_batch_major(o_ref, val, t0, nt)
        bonus_o[steps] = gsum(r * k2 * rk_ref[...]) * v

        xp = jnp.concatenate([conv_s[...], rec_b], axis=0)
        xc = cb_ref[...] + xp[0:nt] * cw_ref[0:1, :]
        for j in range(1, CONV_W):
            xc = xc + xp[j:j + nt] * cw_ref[j:j + 1, :]
        conv_s[...] = xp[nt:nt + CONV_W - 1]

        gates = _mm(_split(xc.reshape(nrows, C_B), precise), wg_ref[...],
                    None if wg_lo is None else wg_lo[...]) + bg_ref[...]
        r_t = _sigmoid(gates[:, :C_B]).reshape(nt, bsz, C_B)
        i_t = _sigmoid(gates[:, C_B:]).reshape(nt, bsz, C_B)
        log_a = -LRU_C * r_t * _softplus(-lam_ref[...])
        a_s[steps] = jnp.exp(log_a)
        u_s[steps] = jnp.sqrt(1.0 - jnp.exp(2.0 * log_a)) * (i_t * xc)
        yb_o[steps] = _gelu_tanh(gate_b)

    n_parts = MIX_PARTS if tt % MIX_PARTS == 0 else 1
    nt = tt // n_parts
    proj = project(0, nt)
    for i in range(n_parts):
        proj_next = project((i + 1) * nt, nt) if i + 1 < n_parts else None
        finish(i * nt, nt, proj)
        proj = proj_next
    conv_o[...] = conv_s[...]

    def scan_step(t, hcur):
        hcur = a_s[t] * hcur + u_s[t]
        hs_s[t] = hcur
        return hcur

    h_fin = lax.fori_loop(0, tt, scan_step, lru_s[...])
    lru_s[...] = h_fin
    lru_o[...] = h_fin
    yb_o[...] = hs_s[...] * yb_o[...]


def _mix_pre_call(x, batch_major, tt, sh1, sc1, h_prev, conv_state, lru_state, wl):
    precise = wl['precise']
    if batch_major:
        bsz, t_len, _ = x.shape
        x_spec = pl.BlockSpec((bsz, tt, D_MODEL), lambda c: (0, c, 0))
    else:
        t_len, bsz, _ = x.shape
        x_spec = pl.BlockSpec((tt, bsz, D_MODEL), lambda c: (c, 0, 0))
    seq = lambda ch: pl.BlockSpec((tt, bsz, ch), lambda c: (c, 0, 0))
    weights = []
    for name in ('w_in', 'w_decay_up', 'w_a_up', 'w_g_up', 'w_gates'):
        weights.append(wl[name])
        if precise:
            weights.append(wl[name + '_lo'])
    weights += [wl['mu'], wl['w0'], wl['a0'], wl['k_k'], wl['k_a'], wl['r_k'], wl['conv_w'], wl['conv_b'],
                wl['b_gates'], wl['lam'], wl['gones']]
    seq_out = jax.ShapeDtypeStruct((t_len, bsz, C_A), F32)
    scan_spec = pl.BlockSpec((bsz, tt, C_A), lambda c: (0, c, 0))
    scan_out = jax.ShapeDtypeStruct((bsz, t_len, C_A), F32)
    scratch = [pltpu.VMEM((bsz, N_SHIFT_PAD), F32), pltpu.VMEM((CONV_W - 1, bsz, C_B), F32),
               pltpu.VMEM((bsz, C_B), F32), pltpu.VMEM((tt, bsz, C_B), F32),
               pltpu.VMEM((tt, bsz, C_B), F32), pltpu.VMEM((tt, bsz, C_B), F32)]
    if batch_major:
        scratch.append(pltpu.VMEM((tt, bsz, D_MODEL), F32))
    return pl.pallas_call(
        functools.partial(_mix_pre_kernel, precise, batch_major, tt, bsz),
        grid=(t_len // tt,),
        in_specs=[x_spec, _full((bsz, D_MODEL)), _full((bsz, D_MODEL)), _full((bsz, D_MODEL)),
                  _full((CONV_W - 1, bsz, C_B)), _full((bsz, C_B))] + [_const(w.shape) for w in weights],
        out_specs=[scan_spec] * 6 + [seq(C_A)] * 3 + [_full((bsz, D_MODEL)), _full((CONV_W - 1, bsz, C_B)),
                                                      _full((bsz, C_B))],
        out_shape=[scan_out] * 6 + [seq_out] * 3 + [jax.ShapeDtypeStruct((bsz, D_MODEL), F32),
                                   jax.ShapeDtypeStruct((CONV_W - 1, bsz, C_B), F32),
                                   jax.ShapeDtypeStruct((bsz, C_B), F32)],
        scratch_shapes=scratch,
        compiler_params=_params(),
        name="mix_pre",
    )(x, sh1, sc1, h_prev, conv_state, lru_state, *weights)


I_LO = HEAD // 2
SUB = 8
N_STRIP = I_LO // SUB


def _strips(ref, *lead):
    return [ref[(*lead, pl.ds(q * SUB, SUB), slice(None))] for q in range(N_STRIP)]


def _row(ref, *idx):
    return jnp.broadcast_to(ref[(*idx, slice(None))], (SUB, 2 * HEAD))


A_OP, W_OP, B_OP, K_OP, R_OP = range(5)
RELAYOUT_UNROLL = 16


V_RAW = 5


def _wkv_scan_kernel(tt, a_hbm, w_hbm, b_hbm, k_hbm, r_hbm, v_hbm, s0_ref,
                     y_ref, sfin_ref, s_s, raw_s, raw_sem, ops_s, zt_s, vop_s, vt_s, ysc_s):
    hbm = (a_hbm, w_hbm, b_hbm, k_hbm, r_hbm, v_hbm)
    bsz = raw_s.shape[1]
    inst = bsz * N_HEADS
    chunk = pl.program_id(0)

    def raw_copy(n, c):
        t0 = pl.multiple_of(c * tt, tt)
        return pltpu.make_async_copy(hbm[n].at[:, pl.ds(t0, tt), :], raw_s.at[n], raw_sem.at[n])

    @pl.when(chunk == 0)
    def _():
        for n in range(len(hbm)):
            raw_copy(n, 0).start()
        s_s[...] = s0_ref[...]

    for n in range(V_RAW):
        raw_copy(n, chunk).wait()
        for bi in range(bsz):
            zt_s[bi * N_HEADS:(bi + 1) * N_HEADS] = raw_s[n, bi].T.reshape(N_HEADS, HEAD, tt)

        def key_rows(j, carry, n=n):
            rows = zt_s[:, j, :]
            ops_s[n, j] = jnp.concatenate([rows, rows], axis=0).T
            return carry

        lax.fori_loop(0, HEAD, key_rows, 0, unroll=RELAYOUT_UNROLL)

    raw_copy(V_RAW, chunk).wait()
    for bi in range(bsz):
        vt_s[bi * N_HEADS:(bi + 1) * N_HEADS] = raw_s[V_RAW, bi].T.reshape(N_HEADS, 2, I_LO, tt)

    def value_rows(il, carry):
        vop_s[:, il, :] = jnp.concatenate([vt_s[:, 0, il, :], vt_s[:, 1, il, :]], axis=0).T
        return carry

    lax.fori_loop(0, I_LO, value_rows, 0, unroll=RELAYOUT_UNROLL)

    @pl.when(chunk + 1 < pl.num_programs(0))
    def _():
        for n in range(len(hbm)):
            raw_copy(n, chunk + 1).start()

    sa = [None] * N_STRIP
    for j in range(HEAD):
        a8 = _row(ops_s, A_OP, j, pl.ds(0, 1))
        for q, s in enumerate(_strips(s_s, j)):
            sa[q] = s * a8 if sa[q] is None else sa[q] + s * a8

    def step(t, sa):
        t_next = jnp.minimum(t + 1, tt - 1)
        v = _strips(vop_s, t)
        y = [None] * N_STRIP
        sa_next = [None] * N_STRIP
        for j in range(HEAD):
            w8, b8, k8, r8 = (_row(ops_s, n, j, pl.ds(t, 1)) for n in (W_OP, B_OP, K_OP, R_OP))
            a8 = _row(ops_s, A_OP, j, pl.ds(t_next, 1))
            for q in range(N_STRIP):
                rows = pl.ds(q * SUB, SUB)
                s_new = s_s[j, rows, :] * w8 + sa[q] * b8 + v[q] * k8
                s_s[j, rows, :] = s_new
                yq, aq = s_new * r8, s_new * a8
                y[q] = yq if y[q] is None else y[q] + yq
                sa_next[q] = aq if sa_next[q] is None else sa_next[q] + aq
        for q in range(N_STRIP):
            ysc_s[t, pl.ds(q * SUB, SUB), :] = y[q]
        return tuple(sa_next)

    lax.fori_loop(0, tt, step, tuple(sa))

    def y_rows(il, carry):
        yt = ysc_s[:, il, :].T
        vt_s[:, 0, il, :] = yt[:inst]
        vt_s[:, 1, il, :] = yt[inst:]
        return carry

    lax.fori_loop(0, I_LO, y_rows, 0, unroll=RELAYOUT_UNROLL)
    for bi in range(bsz):
        y_ref[bi] = vt_s[bi * N_HEADS:(bi + 1) * N_HEADS].reshape(C_A, tt).T

    @pl.when(pl.program_id(0) == pl.num_programs(0) - 1)
    def _():
        sfin_ref[...] = s_s[...]


def _wkv_scan_call(a, w, b, k, r, v, s0, tt):
    bsz, t_len, _ = w.shape
    inst = bsz * N_HEADS
    state = _const((HEAD, I_LO, 2 * HEAD))
    return pl.pallas_call(
        functools.partial(_wkv_scan_kernel, tt),
        grid=(t_len // tt,),
        in_specs=[pl.BlockSpec(memory_space=pl.ANY)] * 6 + [state],
        out_specs=[pl.BlockSpec((bsz, tt, C_A), lambda c: (0, c, 0)), _full((HEAD, I_LO, 2 * HEAD))],
        out_shape=[jax.ShapeDtypeStruct((bsz, t_len, C_A), F32),
                   jax.ShapeDtypeStruct((HEAD, I_LO, 2 * HEAD), F32)],
        scratch_shapes=[pltpu.VMEM((HEAD, I_LO, 2 * HEAD), F32),
                        pltpu.VMEM((6, bsz, tt, C_A), F32),
                        pltpu.SemaphoreType.DMA((6,)),
                        pltpu.VMEM((5, HEAD, tt, 2 * inst), F32),
                        pltpu.VMEM((inst, HEAD, tt), F32),
                        pltpu.VMEM((tt, I_LO, 2 * inst), F32),
                        pltpu.VMEM((inst, 2, I_LO, tt), F32),
                        pltpu.VMEM((tt, I_LO, 2 * inst), F32)],
        compiler_params=_params(),
        name="wkv_scan",
    )(a, w, b, k, r, v, s0)


def _wkv_step_kernel(a_ref, w_ref, b_ref, k_ref, r_ref, v_ref, s_ref, y_ref, so_ref):
    s = s_ref[...]
    row = lax.broadcasted_iota(jnp.int32, (HEAD, HEAD), 0)
    col = lax.broadcasted_iota(jnp.int32, (HEAD, HEAD), 1)
    eye = (row == col).astype(F32)
    sa = jnp.sum(s * a_ref[...], axis=-1, keepdims=True)
    v_col = jnp.sum(eye * v_ref[...], axis=-1, keepdims=True)
    s_new = s * w_ref[...] + sa * b_ref[...] + v_col * k_ref[...]
    so_ref[...] = s_new
    y_col = jnp.sum(s_new * r_ref[...], axis=-1, keepdims=True)
    y_ref[...] = jnp.sum(eye * y_col, axis=-2, keepdims=True)


def _wkv_step_call(a, w, b, k, r, v, s, bb):
    bsz = s.shape[0]
    op = pl.BlockSpec((bb, N_HEADS, 1, HEAD), lambda c: (c, 0, 0, 0))
    st = pl.BlockSpec((bb, N_HEADS, HEAD, HEAD), lambda c: (c, 0, 0, 0))
    return pl.pallas_call(
        _wkv_step_kernel,
        grid=(bsz // bb,),
        in_specs=[op] * 6 + [st],
        out_specs=[op, st],
        out_shape=[jax.ShapeDtypeStruct((bsz, N_HEADS, 1, HEAD), F32),
                   jax.ShapeDtypeStruct((bsz, N_HEADS, HEAD, HEAD), F32)],
        compiler_params=_params(),
        name="wkv_step",
    )(a, w, b, k, r, v, s)


def _route(logits):
    lane = lax.broadcasted_iota(jnp.int32, logits.shape, 1)
    neg = jnp.float32(-jnp.inf)
    big = jnp.int32(ROUTER_LANES)
    is_grp = (lane >= N_EXPERTS) & (lane < N_EXPERTS + N_GROUPS)
    gl = jnp.where(is_grp, logits, neg)
    gmax = jnp.max(gl, axis=-1, keepdims=True)
    ge = jnp.where(is_grp, jnp.exp(gl - gmax), 0.0)
    gp = ge / jnp.sum(ge, axis=-1, keepdims=True)
    gi = jnp.min(jnp.where(gl == gmax, lane, big), axis=-1, keepdims=True)
    p_grp = jnp.sum(jnp.where(lane == gi, gp, 0.0), axis=-1, keepdims=True)
    gidx = gi - N_EXPERTS
    in_grp = (lane >= gidx * EXP_PER_GROUP) & (lane < (gidx + 1) * EXP_PER_GROUP)
    el = jnp.where(in_grp, logits, neg)
    emax = jnp.max(el, axis=-1, keepdims=True)
    ee = jnp.where(in_grp, jnp.exp(el - emax), 0.0)
    pe = ee / jnp.sum(ee, axis=-1, keepdims=True)
    pe_m = jnp.where(in_grp, pe, -1.0)
    v1 = jnp.max(pe_m, axis=-1, keepdims=True)
    i1 = jnp.min(jnp.where(pe_m == v1, lane, big), axis=-1, keepdims=True)
    pe_m2 = jnp.where(lane == i1, -1.0, pe_m)
    v2 = jnp.max(pe_m2, axis=-1, keepdims=True)
    i2 = jnp.min(jnp.where(pe_m2 == v2, lane, big), axis=-1, keepdims=True)
    tot = v1 + v2
    return jnp.where(lane == i1, v1 / tot * p_grp, 0.0) + jnp.where(lane == i2, v2 / tot * p_grp, 0.0)


def _mix_post_kernel(precise, batch_major, tt, bsz,
                     ya_ref, bonus_ref, g_ref, yb_ref, x_ref, g1_ref, sh2_ref, sc2_ref, wout_ref, *refs):
    wout_lo = refs[0] if precise else None
    (lnxg_ref, lnxb_ref, gones_ref, ln1g_ref, ln1b_ref, wrh_ref, wrl_ref, br_ref,
     x1_o, h2_o, comb_o, ya_tm_s, *tm_s) = refs[1:] if precise else refs

    def wout(lo_hi_ref, r0, r1):
        return None if lo_hi_ref is None else lo_hi_ref[r0:r1, :]

    def mix(t0, nt):
        nrows = nt * bsz
        steps = pl.ds(t0, nt)

        def gsum(z):
            return _group_sum(z.reshape(nrows, C_A), gones_ref[...], precise).reshape(nt, bsz, C_A)

        ya = _load_batch_major(ya_ref, ya_tm_s, t0, nt)
        mu = gsum(ya) * (1.0 / HEAD)
        yc = ya - mu
        var = gsum(yc * yc) * (1.0 / HEAD)
        yn = yc * lax.rsqrt(var + GN_EPS) * lnxg_ref[...] + lnxb_ref[...]
        ya = (yn + bonus_ref[steps]) * g_ref[steps]
        return (_mm(_split(ya.reshape(nrows, C_A), precise), wout_ref[0:C_A, :], wout(wout_lo, 0, C_A))
                + _mm(_split(yb_ref[steps].reshape(nrows, C_B), precise), wout_ref[C_A:, :],
                      wout(wout_lo, C_A, C_A + C_B))).reshape(nt, bsz, D_MODEL)

    def post(t0, nt, y):
        nrows = nt * bsz
        steps = pl.ds(t0, nt)
        out_rows = pl.ds(t0 * bsz, nrows)
        x = _load_batch_major(x_ref, tm_s[0], t0, nt) if batch_major else x_ref[steps]
        x1 = _layer_norm(ALPHA * x + (1.0 + g1_ref[...]) * y) * ln1g_ref[...] + ln1b_ref[...]
        x1_o[steps] = x1
        h2 = (_layer_norm(x1) * (1.0 + sc2_ref[...]) + sh2_ref[...]).reshape(nrows, D_MODEL)
        hi = h2.astype(BF16)
        lo = (h2 - hi.astype(F32)).astype(BF16)
        h2_o[out_rows, :] = h2 if h2_o.dtype == F32 else hi
        logits = _dot(hi, wrh_ref[...]) + _dot(lo, wrh_ref[...]) + _dot(hi, wrl_ref[...]) + br_ref[...]
        comb_o[out_rows, :] = _route(logits)

    n_parts = MIX_PARTS if tt % MIX_PARTS == 0 else 1
    nt = tt // n_parts
    y = mix(0, nt)
    for i in range(n_parts):
        y_next = mix((i + 1) * nt, nt) if i + 1 < n_parts else None
        post(i * nt, nt, y)
        y = y_next


def _mix_post_call(ya, bonus, g, yb, x, batch_major, tt, g1, sh2, sc2, wl):
    bsz, t_len, _ = ya.shape
    rows = tt * bsz
    ya_spec = pl.BlockSpec((bsz, tt, C_A), lambda c: (0, c, 0))
    if batch_major:
        x_spec = pl.BlockSpec((bsz, tt, D_MODEL), lambda c: (0, c, 0))
    else:
        x_spec = pl.BlockSpec((tt, bsz, D_MODEL), lambda c: (c, 0, 0))
    seq = pl.BlockSpec((tt, bsz, C_A), lambda c: (c, 0, 0))
    precise = wl['precise']
    weights = [wl['w_out']] + ([wl['w_out_lo']] if precise else []) + [
        wl['lnx_gain'], wl['lnx_bias'], wl['gones'], wl['ln1_gain'], wl['ln1_bias'],
        wl['w_router_hi'], wl['w_router_lo'], wl['b_router']]
    scratch = [pltpu.VMEM((tt, bsz, C_A), F32)] + ([pltpu.VMEM((tt, bsz, D_MODEL), F32)] if batch_major else [])
    return pl.pallas_call(
        functools.partial(_mix_post_kernel, precise, batch_major, tt, bsz),
        grid=(t_len // tt,),
        in_specs=[ya_spec] + [seq] * 3 + [x_spec] + [_full((bsz, D_MODEL))] * 3 + [_const(w.shape) for w in weights],
        out_specs=[pl.BlockSpec((tt, bsz, D_MODEL), lambda c: (c, 0, 0)),
                   pl.BlockSpec((rows, D_MODEL), lambda c: (c, 0)),
                   pl.BlockSpec((rows, ROUTER_LANES), lambda c: (c, 0))],
        out_shape=[jax.ShapeDtypeStruct((t_len, bsz, D_MODEL), F32),
                   jax.ShapeDtypeStruct((t_len * bsz, D_MODEL), F32 if wl['moe_precise'] else BF16),
                   jax.ShapeDtypeStruct((t_len * bsz, ROUTER_LANES), F32)],
        scratch_shapes=scratch,
        compiler_params=_params(),
        name="mix_post",
    )(ya, bonus, g, yb, x, g1, sh2, sc2, *weights)


def _moe_kernel(precise, batch_major_out, tt, bsz, h2_ref, comb_ref, x1_ref, g2_ref, *refs):
    wg_ref, wu_ref, wd_ref, ln2g_ref, ln2b_ref, o_ref, acc_s = refs
    weight = (lambda r, i: _split(r[i])) if precise else (lambda r, i: (r[i], None))
    step = pl.program_id(1)
    rows = tt * bsz

    @pl.when(step == 0)
    def _():
        acc_s[...] = jnp.zeros_like(acc_s)

    h = _split(h2_ref[...], True) if precise else (h2_ref[...], None)
    lane = lax.broadcasted_iota(jnp.int32, (rows, ROUTER_LANES), 1)
    comb = comb_ref[...]
    down = None
    for i in range(MOE_EXPERTS_PER_STEP):
        ce = jnp.sum(jnp.where(lane == step * MOE_EXPERTS_PER_STEP + i, comb, 0.0), axis=-1, keepdims=True)
        gate = _mm(h, *weight(wg_ref, i))
        hid = (gate * _sigmoid(gate)) * _mm(h, *weight(wu_ref, i))
        part = _mm(_split(hid * ce, precise), *weight(wd_ref, i))
        down = part if down is None else down + part
    acc_s[...] += down

    @pl.when(step == N_EXPERTS // MOE_EXPERTS_PER_STEP - 1)
    def _():
        moe = acc_s[...].reshape(tt, bsz, D_MODEL)
        out = _layer_norm(ALPHA * x1_ref[...] + (1.0 + g2_ref[...]) * moe) * ln2g_ref[...] + ln2b_ref[...]
        if batch_major_out:
            _store_batch_major(o_ref, out, 0, tt)
        else:
            o_ref[...] = out


def _moe_call(h2, comb, x1, batch_major_out, tt, g2, wl):
    precise = wl['moe_precise']
    t_len, bsz, _ = x1.shape
    rows = tt * bsz
    first = wl['layer'] * (N_EXPERTS // MOE_EXPERTS_PER_STEP) if precise else 0
    w_gate = pl.BlockSpec((MOE_EXPERTS_PER_STEP, D_MODEL, D_EXPERT), lambda c, e: (first + e, 0, 0))
    w_down = pl.BlockSpec((MOE_EXPERTS_PER_STEP, D_EXPERT, D_MODEL), lambda c, e: (first + e, 0, 0))
    names = ('w_exp_gate', 'w_exp_up', 'w_exp_down')
    weights = [wl[name + ('_f32' if precise else '')] for name in names]
    w_specs = [w_gate, w_gate, w_down]
    if batch_major_out:
        o_spec = pl.BlockSpec((bsz, tt, D_MODEL), lambda c, e: (0, c, 0))
        o_shape = jax.ShapeDtypeStruct((bsz, t_len, D_MODEL), F32)
    else:
        o_spec = pl.BlockSpec((tt, bsz, D_MODEL), lambda c, e: (c, 0, 0))
        o_shape = jax.ShapeDtypeStruct((t_len, bsz, D_MODEL), F32)
    return pl.pallas_call(
        functools.partial(_moe_kernel, precise, batch_major_out, tt, bsz),
        grid=(t_len // tt, N_EXPERTS // MOE_EXPERTS_PER_STEP),
        in_specs=[pl.BlockSpec((rows, D_MODEL), lambda c, e: (c, 0)),
                  pl.BlockSpec((rows, ROUTER_LANES), lambda c, e: (c, 0)),
                  pl.BlockSpec((tt, bsz, D_MODEL), lambda c, e: (c, 0, 0)),
                  pl.BlockSpec((bsz, D_MODEL), lambda c, e: (0, 0))] + w_specs + [
                  pl.BlockSpec((1, D_MODEL), lambda c, e: (0, 0)),
                  pl.BlockSpec((1, D_MODEL), lambda c, e: (0, 0))],
        out_specs=o_spec,
        out_shape=o_shape,
        scratch_shapes=[pltpu.VMEM((rows, D_MODEL), F32)],
        compiler_params=_params(2),
        name="moe",
    )(h2, comb, x1, g2, *weights, wl['ln2_gain'], wl['ln2_bias'])


def _pad_cols(w, width):
    return jnp.pad(w, ((0, 0), (0, width - w.shape[1])))


def _pad_rows(w, height):
    return jnp.pad(w, ((0, height - w.shape[0]), (0, 0)))


def _split_weight_kernel(w_ref, hi_ref, lo_ref):
    hi_ref[...], lo_ref[...] = _split(w_ref[...])


def _split_weight(w):
    w2 = w.reshape(-1, w.shape[-1])
    rows, cols = w2.shape
    tr = min(rows, 512)
    spec = pl.BlockSpec((tr, cols), lambda i: (i, 0))
    hi, lo = pl.pallas_call(
        _split_weight_kernel,
        grid=(rows // tr,),
        in_specs=[spec],
        out_specs=[spec, spec],
        out_shape=[jax.ShapeDtypeStruct(w2.shape, BF16)] * 2,
        compiler_params=_params(),
        name="split_weight",
    )(w2)
    return hi.reshape(w.shape), lo.reshape(w.shape)


def _block_diag(w):
    n, c, d = w.shape
    eye = jnp.eye(n, dtype=w.dtype)
    return (eye[:, None, :, None] * w[:, :, None, :]).reshape(n * c, n * d)


def _prep_layer(p, l, precise, moe_precise):
    row = lambda v: v[l].reshape(1, -1)
    w_in = p['w_in'][l]
    o1, o2, o3 = 3 * C_A, 3 * C_A + LORA_DECAY, 3 * C_A + LORA_DECAY + LORA_AAA
    pieces = [(0, o1, o1), (o1, o2, LORA_PAD), (o2, o3, LORA_PAD), (o3, N_SHIFT, LORA_PAD)]
    w_in_p = jnp.concatenate([_pad_cols(w_in[:, a:b], wd) for a, b, wd in pieces] + [w_in[:, N_SHIFT:]], axis=1)
    mu = p['mu_shift'][l].reshape(1, -1)
    mu_p = jnp.concatenate([_pad_cols(mu[:, a:b], wd) for a, b, wd in pieces], axis=1)
    w_router = jnp.concatenate([p['w_router_expert'][l], p['w_router_group'][l]], axis=1)
    w_router = _pad_cols(w_router, ROUTER_LANES)
    w_router_hi, w_router_lo = _split_weight(w_router)
    b_router = _pad_cols(jnp.concatenate([p['b_router_expert'][l], p['b_router_group'][l]]).reshape(1, -1),
                         ROUTER_LANES)
    head_id = jnp.arange(C_A) // HEAD
    mats = dict(
        w_in=w_in_p,
        w_decay_up=_pad_rows(p['w_decay_up'][l], LORA_PAD),
        w_a_up=_pad_rows(p['w_a_up'][l], LORA_PAD),
        w_g_up=_pad_rows(p['w_g_up'][l], LORA_PAD),
        w_gates=jnp.concatenate([_block_diag(p['w_rgate'][l]), _block_diag(p['w_igate'][l])], axis=1),
        w_out=p['w_out'][l])
    split_mats = {}
    for name, w in mats.items():
        if precise:
            split_mats[name], split_mats[name + '_lo'] = _split_weight(w)
        else:
            split_mats[name] = w.astype(BF16)
    for name in ('w_exp_gate', 'w_exp_up', 'w_exp_down'):
        split_mats[name] = p[name][l].astype(BF16)
        split_mats[name + '_f32'] = p[name].reshape((-1,) + p[name].shape[2:])
    return dict(
        split_mats, layer=l, precise=precise, moe_precise=moe_precise, mu=mu_p, w0=row(p['w0']), a0=row(p['a0']),
        k_k=row(p['k_k']), k_a=row(p['k_a']), r_k=row(p['r_k']),
        conv_w=p['conv_w'][l], conv_b=row(p['conv_b']),
        b_gates=jnp.concatenate([p['b_rgate'][l], p['b_igate'][l]]).reshape(1, -1),
        lam=row(p['lru_lambda']),
        gones=(head_id[:, None] == head_id[None, :]).astype(BF16),
        lnx_gain=row(p['lnx_gain']), lnx_bias=row(p['lnx_bias']),
        ln1_gain=row(p['ln1_gain']), ln1_bias=row(p['ln1_bias']),
        w_router_hi=w_router_hi, w_router_lo=w_router_lo,
        b_router=b_router,
        ln2_gain=row(p['ln2_gain']), ln2_bias=row(p['ln2_bias']),
    )


def _trunk(x, batch_major, tt, mods, st_wkv, st_shift, st_conv, st_lru, layers, moe_tt, scan_tt):
    wkv_out, shift_out, conv_out, lru_out = [], [], [], []
    for l in range(DEPTH):
        wl = layers[l]
        sh1, sc1, g1, sh2, sc2, g2 = mods[l]
        bm_in = batch_major and l == 0
        conv_state = jnp.swapaxes(st_conv[l], 0, 1)
        (r, w, k, v, a, b, g, bonus, yb, shift_new, conv_new, lru_new) = _mix_pre_call(
            x, bm_in, tt, sh1, sc1, st_shift[l], conv_state, st_lru[l], wl)
        bsz, t_len, _ = r.shape
        if t_len > 1:
            s0 = st_wkv[l].reshape(bsz, N_HEADS, 2, I_LO, HEAD).transpose(4, 3, 2, 0, 1)
            s0 = s0.reshape(HEAD, I_LO, 2 * HEAD)
            ya, s_fin = _wkv_scan_call(a, w, b, k, r, v, s0, scan_tt)
            s_new = s_fin.reshape(HEAD, I_LO, 2, bsz, N_HEADS).transpose(3, 4, 2, 1, 0)
            s_new = s_new.reshape(bsz, N_HEADS, HEAD, HEAD)
        else:
            shp = (bsz, N_HEADS, 1, HEAD)
            y4, s_new = _wkv_step_call(a.reshape(shp), w.reshape(shp), b.reshape(shp), k.reshape(shp),
                                       r.reshape(shp), v.reshape(shp), st_wkv[l], WKV_STEP_BATCH)
            ya = y4.reshape(bsz, 1, C_A)
        x1, h2, comb = _mix_post_call(ya, bonus, g, yb, x, bm_in, tt, g1, sh2, sc2, wl)
        bm_out = batch_major and l == DEPTH - 1
        x = _moe_call(h2, comb, x1, bm_out, moe_tt, g2, wl)
        wkv_out.append(s_new)
        shift_out.append(shift_new)
        conv_out.append(jnp.swapaxes(conv_new, 0, 1))
        lru_out.append(lru_new)
    return x, jnp.stack(wkv_out), jnp.stack(shift_out), jnp.stack(conv_out), jnp.stack(lru_out)


def kernel(x_prompt, x_sample, c_prompt, c_sample, state_wkv, state_shift, state_conv, state_lru, w_ada, b_ada, w_in, mu_shift, w0, w_decay_up, a0, w_a_up, w_g_up, k_k, k_a, r_k, lnx_gain, lnx_bias, conv_w, conv_b, w_rgate, b_rgate, w_igate, b_igate, lru_lambda, w_out, ln1_gain, ln1_bias, w_router_group, b_router_group, w_router_expert, b_router_expert, w_exp_gate, w_exp_up, w_exp_down, ln2_gain, ln2_bias):
    p = dict(w_in=w_in, mu_shift=mu_shift, w0=w0, w_decay_up=w_decay_up, a0=a0, w_a_up=w_a_up,
             w_g_up=w_g_up, k_k=k_k, k_a=k_a, r_k=r_k.reshape(DEPTH, C_A), lnx_gain=lnx_gain,
             lnx_bias=lnx_bias, conv_w=conv_w, conv_b=conv_b, w_rgate=w_rgate, b_rgate=b_rgate,
             w_igate=w_igate, b_igate=b_igate, lru_lambda=lru_lambda, w_out=w_out, ln1_gain=ln1_gain,
             ln1_bias=ln1_bias, w_router_group=w_router_group, b_router_group=b_router_group,
             w_router_expert=w_router_expert, b_router_expert=b_router_expert, w_exp_gate=w_exp_gate,
             w_exp_up=w_exp_up, w_exp_down=w_exp_down, ln2_gain=ln2_gain, ln2_bias=ln2_bias)
    prepared = [_prep_layer(p, l, precise=True, moe_precise=(l < DEPTH - 1)) for l in range(DEPTH)]
    layers_s = prepared
    layers_p = [dict(prepared[l], precise=(l == 0), moe_precise=False) for l in range(DEPTH)]
    bp, bs = x_prompt.shape[0], x_sample.shape[0]

    mod = _ada_call(jnp.concatenate([c_sample, c_prompt], axis=0), w_ada, b_ada)
    split = lambda m: [m[:, i * D_MODEL:(i + 1) * D_MODEL] for i in range(6)]
    mods_s = [split(mod[l, :bs]) for l in range(DEPTH)]
    mods_p = [split(mod[l, bs:bs + bp]) for l in range(DEPTH)]

    z_wkv = jnp.zeros((DEPTH, bp, N_HEADS, HEAD, HEAD), F32)
    z_shift = jnp.zeros((DEPTH, bp, D_MODEL), F32)
    z_conv = jnp.zeros((DEPTH, bp, CONV_W - 1, C_B), F32)
    z_lru = jnp.zeros((DEPTH, bp, C_B), F32)
    y_p, wkv_p, shift_p, conv_p, lru_p = _trunk(x_prompt, True, MIX_TT, mods_p, z_wkv, z_shift, z_conv, z_lru,
                                                layers_p, MOE_TT, SCAN_TT)
    xs_tm = x_sample.reshape(1, bs, D_MODEL)
    y_s, wkv_s, shift_s, conv_s, lru_s = _trunk(xs_tm, False, 1, mods_s, state_wkv, state_shift, state_conv,
                                                state_lru, layers_s, 1, 1)
    return (y_p, y_s.reshape(bs, 1, D_MODEL), wkv_p, shift_p, conv_p, lru_p, wkv_s, shift_s, conv_s, lru_s)
```

```python
import functools

import jax
import jax.numpy as jnp
from jax import lax
from jax.experimental import pallas as pl
from jax.experimental.pallas import tpu as pltpu

F32 = jnp.float32
BF16 = jnp.bfloat16

D_MODEL = 1024
DEPTH = 2
C_A = 512
C_B = 512
HEAD = 64
N_HEADS = C_A // HEAD
LORA_DECAY = 32
LORA_AAA = 32
LORA_GATE = 96
LORA_PAD = 128
N_SHIFT = 3 * C_A + LORA_DECAY + LORA_AAA + LORA_GATE
N_SHIFT_PAD = 3 * C_A + 3 * LORA_PAD
N_IN_PAD = N_SHIFT_PAD + 2 * C_B
CONV_W = 4
LRU_BLOCKS = 8
LRU_C = 8.0
N_GROUPS = 4
EXP_PER_GROUP = 4
N_EXPERTS = 16
MOE_EXPERTS_PER_STEP = 4
D_EXPERT = 256
ROUTER_LANES = 128
ALPHA = (2 * DEPTH) ** 0.25
LN_EPS = 1e-5
GN_EPS = 64e-5
VMEM_LIMIT = 56 * 1024 * 1024
MOE_VMEM_LIMIT = 62 * 1024 * 1024
MIX_TT = 64
MOE_TT = 128
SCAN_TT = 128
WKV_STEP_BATCH = 16


def _params(n_axes=1, vmem_limit=VMEM_LIMIT):
    return pltpu.CompilerParams(dimension_semantics=("arbitrary",) * n_axes,
                                vmem_limit_bytes=vmem_limit)


def _full(shape):
    return pl.BlockSpec(shape, lambda *_: (0,) * len(shape))


def _const(shape):
    return pl.BlockSpec(shape, lambda *_: (0,) * len(shape), pipeline_mode=pl.Buffered(1))


def _layer_norm(x):
    mu = jnp.mean(x, axis=-1, keepdims=True)
    xc = x - mu
    var = jnp.mean(xc * xc, axis=-1, keepdims=True)
    return xc * lax.rsqrt(var + LN_EPS)


def _softplus(z):
    return jnp.maximum(z, 0.0) + jnp.log1p(jnp.exp(-jnp.abs(z)))


def _sigmoid(z):
    return 1.0 / (1.0 + jnp.exp(-z))


def _gelu_tanh(x):
    c = 0.7978845608028654
    return x * (0.5 * (1.0 + jnp.tanh(c * (x + 0.044715 * (x * x * x)))))


def _dot(a, b):
    return jnp.dot(a, b, preferred_element_type=F32)


def _split(x, precise=True):
    hi = x.astype(BF16)
    return hi, ((x - hi.astype(F32)).astype(BF16) if precise else None)


def _mm(xs, w_hi, w_lo=None):
    hi, lo = xs
    out = _dot(hi, w_hi)
    if w_lo is not None:
        out = out + _dot(lo, w_hi) + _dot(hi, w_lo)
    return out


def _group_sum(x, ones_bf16, precise):
    hi, lo = _split(x, precise)
    out = _dot(hi, ones_bf16)
    return out + _dot(lo, ones_bf16) if precise else out


def _load_batch_major(x_ref, tm_ref, t0, nt):
    for t in range(t0, t0 + nt):
        tm_ref[t] = x_ref[:, t, :]
    return tm_ref[pl.ds(t0, nt)]


def _store_batch_major(o_ref, val, t0, nt):
    for t in range(nt):
        o_ref[:, t0 + t, :] = val[t]


def _ada_kernel(c_ref, w_ref, b_ref, o_ref):
    c = c_ref[...]
    w_hi, w_lo = _split(w_ref[0])
    o_ref[0] = _mm(_split(c * _sigmoid(c)), w_hi, w_lo) + b_ref[0]


def _ada_call(c_all, w_ada, b_ada):
    n = c_all.shape[0]
    tn = 1536
    return pl.pallas_call(
        _ada_kernel,
        grid=(DEPTH, 6 * D_MODEL // tn),
        in_specs=[pl.BlockSpec((n, D_MODEL), lambda l, j: (0, 0)),
                  pl.BlockSpec((1, D_MODEL, tn), lambda l, j: (l, 0, j)),
                  pl.BlockSpec((1, 1, tn), lambda l, j: (l, 0, j))],
        out_specs=pl.BlockSpec((1, n, tn), lambda l, j: (l, 0, j)),
        out_shape=jax.ShapeDtypeStruct((DEPTH, n, 6 * D_MODEL), F32),
        compiler_params=_params(2),
        name="ada_mod",
    )(c_all, w_ada, b_ada.reshape(DEPTH, 1, 6 * D_MODEL))


MIX_PARTS = 2
N_PRE_MATS = 5


def _mix_pre_kernel(precise, batch_major, tt, bsz,
                    x_ref, sh_ref, sc_ref, hprev_ref, cst_ref, lst_ref, *refs):
    n_mat = N_PRE_MATS * (2 if precise else 1)
    mats = refs[:n_mat]
    if precise:
        (win_ref, win_lo), (wdu_ref, wdu_lo), (wau_ref, wau_lo), (wgu_ref, wgu_lo), (wg_ref, wg_lo) = (
            (mats[2 * i], mats[2 * i + 1]) for i in range(N_PRE_MATS))
    else:
        (win_ref, win_lo), (wdu_ref, wdu_lo), (wau_ref, wau_lo), (wgu_ref, wgu_lo), (wg_ref, wg_lo) = (
            (m, None) for m in mats)
    (mu_ref, w0_ref, a0_ref, kk_ref, ka_ref, rk_ref, cw_ref, cb_ref, bg_ref, lam_ref, gones_ref,
     r_o, w_o, k_o, v_o, a_o, b_o, g_o, bonus_o, yb_o, shift_o, conv_o, lru_o,
     prevp_s, conv_s, lru_s, a_s, u_s, hs_s, *tm_s) = refs[n_mat:]

    def win(lo_hi_ref, c0, c1):
        return None if lo_hi_ref is None else lo_hi_ref[:, c0:c1]

    @pl.when(pl.program_id(0) == 0)
    def _():
        prevp_s[...] = _mm(_split(hprev_ref[...], precise), win_ref[:, :N_SHIFT_PAD], win(win_lo, 0, N_SHIFT_PAD))
        conv_s[...] = cst_ref[...]
        lru_s[...] = lst_ref[...]

    col_blocks = ((0, C_A), (C_A, C_A), (2 * C_A, C_A), (3 * C_A, 3 * LORA_PAD),
                  (N_SHIFT_PAD, C_B), (N_SHIFT_PAD + C_B, C_B))

    def project(t0, nt):
        nrows = nt * bsz
        x = _load_batch_major(x_ref, tm_s[0], t0, nt) if batch_major else x_ref[pl.ds(t0, nt)]
        h = _layer_norm(x) * (1.0 + sc_ref[...]) + sh_ref[...]
        if t0 + nt == tt:
            shift_o[...] = h[nt - 1]
        hb = _split(h.reshape(nrows, D_MODEL), precise)
        return [_mm(hb, win_ref[:, off:off + width], win(win_lo, off, off + width)).reshape(nt, bsz, width)
                for off, width in col_blocks]

    def finish(t0, nt, proj):
        nrows = nt * bsz
        steps = pl.ds(t0, nt)

        def shifted(idx):
            off, width = col_blocks[idx]
            p = proj[idx]
            first = prevp_s[:, off:off + width][None]
            prev = jnp.concatenate([first, p[:nt - 1]], axis=0) if nt > 1 else first
            prevp_s[:, off:off + width] = p[nt - 1]
            return p + (prev - p) * mu_ref[:, off:off + width]

        def up(z, w_ref_, w_lo_):
            zs = _split(z.reshape(nrows, LORA_PAD), precise)
            return _mm(zs, w_ref_[...], None if w_lo_ is None else w_lo_[...]).reshape(nt, bsz, C_A)

        def gsum(z):
            return _group_sum(z.reshape(nrows, C_A), gones_ref[...], precise).reshape(nt, bsz, C_A)

        r, k, v, lora = (shifted(idx) for idx in range(4))
        gate_b, rec_b = proj[4], proj[5]
        wd = lora[:, :, 0:LORA_PAD]
        ad = lora[:, :, LORA_PAD:2 * LORA_PAD]
        gd = lora[:, :, 2 * LORA_PAD:3 * LORA_PAD]

        w_log = -_softplus(-(w0_ref[...] + up(jnp.tanh(wd), wdu_ref, wdu_lo))) - 0.5
        decay = jnp.exp(-jnp.exp(w_log))
        a_sig = _sigmoid(a0_ref[...] + up(ad, wau_ref, wau_lo))
        g_o[steps] = up(_sigmoid(gd), wgu_ref, wgu_lo)

        kk = k * kk_ref[...]
        kk = kk / jnp.maximum(jnp.sqrt(gsum(kk * kk)), 1e-12)
        k2 = k * (1.0 + (a_sig - 1.0) * ka_ref[...])
        for o_ref, val in ((r_o, r), (w_o, decay), (k_o, k2), (v_o, v), (a_o, -kk), (b_o, kk * a_sig)):
            _store_batch_major(o_ref, val, t0, nt)
        bonus_o[steps] = gsum(r * k2 * rk_ref[...]) * v

        xp = jnp.concatenate([conv_s[...], rec_b], axis=0)
        xc = cb_ref[...] + xp[0:nt] * cw_ref[0:1, :]
        for j in range(1, CONV_W):
            xc = xc + xp[j:j + nt] * cw_ref[j:j + 1, :]
        conv_s[...] = xp[nt:nt + CONV_W - 1]

        gates = _mm(_split(xc.reshape(nrows, C_B), precise), wg_ref[...],
                    None if wg_lo is None else wg_lo[...]) + bg_ref[...]
        r_t = _sigmoid(gates[:, :C_B]).reshape(nt, bsz, C_B)
        i_t = _sigmoid(gates[:, C_B:]).reshape(nt, bsz, C_B)
        log_a = -LRU_C * r_t * _softplus(-lam_ref[...])
        a_s[steps] = jnp.exp(log_a)
        u_s[steps] = jnp.sqrt(1.0 - jnp.exp(2.0 * log_a)) * (i_t * xc)
        yb_o[steps] = _gelu_tanh(gate_b)

    n_parts = MIX_PARTS if tt % MIX_PARTS == 0 else 1
    nt = tt // n_parts
    proj = project(0, nt)
    for i in range(n_parts):
        proj_next = project((i + 1) * nt, nt) if i + 1 < n_parts else None
        finish(i * nt, nt, proj)
        proj = proj_next
    conv_o[...] = conv_s[...]

    def scan_step(t, hcur):
        hcur = a_s[t] * hcur + u_s[t]
        hs_s[t] = hcur
        return hcur

    h_fin = lax.fori_loop(0, tt, scan_step, lru_s[...])
    lru_s[...] = h_fin
    lru_o[...] = h_fin
    yb_o[...] = hs_s[...] * yb_o[...]


def _mix_pre_call(x, batch_major, tt, sh1, sc1, h_prev, conv_state, lru_state, wl):
    precise = wl['precise']
    if batch_major:
        bsz, t_len, _ = x.shape
        x_spec = pl.BlockSpec((bsz, tt, D_MODEL), lambda c: (0, c, 0))
    else:
        t_len, bsz, _ = x.shape
        x_spec = pl.BlockSpec((tt, bsz, D_MODEL), lambda c: (c, 0, 0))
    seq = lambda ch: pl.BlockSpec((tt, bsz, ch), lambda c: (c, 0, 0))
    weights = []
    for name in ('w_in', 'w_decay_up', 'w_a_up', 'w_g_up', 'w_gates'):
        weights.append(wl[name])
        if precise:
            weights.append(wl[name + '_lo'])
    weights += [wl['mu'], wl['w0'], wl['a0'], wl['k_k'], wl['k_a'], wl['r_k'], wl['conv_w'], wl['conv_b'],
                wl['b_gates'], wl['lam'], wl['gones']]
    seq_out = jax.ShapeDtypeStruct((t_len, bsz, C_A), F32)
    scan_spec = pl.BlockSpec((bsz, tt, C_A), lambda c: (0, c, 0))
    scan_out = jax.ShapeDtypeStruct((bsz, t_len, C_A), F32)
    scratch = [pltpu.VMEM((bsz, N_SHIFT_PAD), F32), pltpu.VMEM((CONV_W - 1, bsz, C_B), F32),
               pltpu.VMEM((bsz, C_B), F32), pltpu.VMEM((tt, bsz, C_B), F32),
               pltpu.VMEM((tt, bsz, C_B), F32), pltpu.VMEM((tt, bsz, C_B), F32)]
    if batch_major:
        scratch.append(pltpu.VMEM((tt, bsz, D_MODEL), F32))
    return pl.pallas_call(
        functools.partial(_mix_pre_kernel, precise, batch_major, tt, bsz),
        grid=(t_len // tt,),
        in_specs=[x_spec, _full((bsz, D_MODEL)), _full((bsz, D_MODEL)), _full((bsz, D_MODEL)),
                  _full((CONV_W - 1, bsz, C_B)), _full((bsz, C_B))] + [_const(w.shape) for w in weights],
        out_specs=[scan_spec] * 6 + [seq(C_A)] * 3 + [_full((bsz, D_MODEL)), _full((CONV_W - 1, bsz, C_B)),
                                                      _full((bsz, C_B))],
        out_shape=[scan_out] * 6 + [seq_out] * 3 + [jax.ShapeDtypeStruct((bsz, D_MODEL), F32),
                                   jax.ShapeDtypeStruct((CONV_W - 1, bsz, C_B), F32),
                                   jax.ShapeDtypeStruct((bsz, C_B), F32)],
        scratch_shapes=scratch,
        compiler_params=_params(),
        name="mix_pre",
    )(x, sh1, sc1, h_prev, conv_state, lru_state, *weights)


I_LO = HEAD // 2
SUB = 8
N_STRIP = I_LO // SUB


def _strips(ref, *lead):
    return [ref[(*lead, pl.ds(q * SUB, SUB), slice(None))] for q in range(N_STRIP)]


def _row(ref, *idx):
    return jnp.broadcast_to(ref[(*idx, slice(None))], (SUB, 2 * HEAD))


A_OP, W_OP, B_OP, K_OP, R_OP = range(5)
RELAYOUT_UNROLL = 16


V_RAW = 5


def _wkv_scan_kernel(tt, a_hbm, w_hbm, b_hbm, k_hbm, r_hbm, v_hbm, s0_ref,
                     y_ref, sfin_ref, s_s, raw_s, raw_sem, ops_s, zt_s, vop_s, vt_s, ysc_s):
    hbm = (a_hbm, w_hbm, b_hbm, k_hbm, r_hbm, v_hbm)
    bsz = raw_s.shape[1]
    inst = bsz * N_HEADS
    chunk = pl.program_id(0)

    def raw_copy(n, c):
        t0 = pl.multiple_of(c * tt, tt)
        return pltpu.make_async_copy(hbm[n].at[:, pl.ds(t0, tt), :], raw_s.at[n], raw_sem.at[n])

    @pl.when(chunk == 0)
    def _():
        for n in range(len(hbm)):
            raw_copy(n, 0).start()
        s_s[...] = s0_ref[...]

    for n in range(V_RAW):
        raw_copy(n, chunk).wait()
        for bi in range(bsz):
            zt_s[bi * N_HEADS:(bi + 1) * N_HEADS] = raw_s[n, bi].T.reshape(N_HEADS, HEAD, tt)

        def key_rows(j, carry, n=n):
            rows = zt_s[:, j, :]
            ops_s[n, j] = jnp.concatenate([rows, rows], axis=0).T
            return carry

        lax.fori_loop(0, HEAD, key_rows, 0, unroll=RELAYOUT_UNROLL)

    raw_copy(V_RAW, chunk).wait()
    for bi in range(bsz):
        vt_s[bi * N_HEADS:(bi + 1) * N_HEADS] = raw_s[V_RAW, bi].T.reshape(N_HEADS, 2, I_LO, tt)

    def value_rows(il, carry):
        vop_s[:, il, :] = jnp.concatenate([vt_s[:, 0, il, :], vt_s[:, 1, il, :]], axis=0).T
        return carry

    lax.fori_loop(0, I_LO, value_rows, 0, unroll=RELAYOUT_UNROLL)

    @pl.when(chunk + 1 < pl.num_programs(0))
    def _():
        for n in range(len(hbm)):
            raw_copy(n, chunk + 1).start()

    sa = [None] * N_STRIP
    for j in range(HEAD):
        a8 = _row(ops_s, A_OP, j, pl.ds(0, 1))
        for q, s in enumerate(_strips(s_s, j)):
            sa[q] = s * a8 if sa[q] is None else sa[q] + s * a8

    def step(t, sa):
        t_next = jnp.minimum(t + 1, tt - 1)
        v = _strips(vop_s, t)
        y = [None] * N_STRIP
        sa_next = [None] * N_STRIP
        for j in range(HEAD):
            w8, b8, k8, r8 = (_row(ops_s, n, j, pl.ds(t, 1)) for n in (W_OP, B_OP, K_OP, R_OP))
            a8 = _row(ops_s, A_OP, j, pl.ds(t_next, 1))
            for q in range(N_STRIP):
                rows = pl.ds(q * SUB, SUB)
                s_new = s_s[j, rows, :] * w8 + sa[q] * b8 + v[q] * k8
                s_s[j, rows, :] = s_new
                yq, aq = s_new * r8, s_new * a8
                y[q] = yq if y[q] is None else y[q] + yq
                sa_next[q] = aq if sa_next[q] is None else sa_next[q] + aq
        for q in range(N_STRIP):
            ysc_s[t, pl.ds(q * SUB, SUB), :] = y[q]
        return tuple(sa_next)

    lax.fori_loop(0, tt, step, tuple(sa))

    def y_rows(il, carry):
        yt = ysc_s[:, il, :].T
        vt_s[:, 0, il, :] = yt[:inst]
        vt_s[:, 1, il, :] = yt[inst:]
        return carry

    lax.fori_loop(0, I_LO, y_rows, 0, unroll=RELAYOUT_UNROLL)
    for bi in range(bsz):
        y_ref[bi] = vt_s[bi * N_HEADS:(bi + 1) * N_HEADS].reshape(C_A, tt).T

    @pl.when(pl.program_id(0) == pl.num_programs(0) - 1)
    def _():
        sfin_ref[...] = s_s[...]


def _wkv_scan_call(a, w, b, k, r, v, s0, tt):
    bsz, t_len, _ = w.shape
    inst = bsz * N_HEADS
    state = _const((HEAD, I_LO, 2 * HEAD))
    return pl.pallas_call(
        functools.partial(_wkv_scan_kernel, tt),
        grid=(t_len // tt,),
        in_specs=[pl.BlockSpec(memory_space=pl.ANY)] * 6 + [state],
        out_specs=[pl.BlockSpec((bsz, tt, C_A), lambda c: (0, c, 0)), _full((HEAD, I_LO, 2 * HEAD))],
        out_shape=[jax.ShapeDtypeStruct((bsz, t_len, C_A), F32),
                   jax.ShapeDtypeStruct((HEAD, I_LO, 2 * HEAD), F32)],
        scratch_shapes=[pltpu.VMEM((HEAD, I_LO, 2 * HEAD), F32),
                        pltpu.VMEM((6, bsz, tt, C_A), F32),
                        pltpu.SemaphoreType.DMA((6,)),
                        pltpu.VMEM((5, HEAD, tt, 2 * inst), F32),
                        pltpu.VMEM((inst, HEAD, tt), F32),
                        pltpu.VMEM((tt, I_LO, 2 * inst), F32),
                        pltpu.VMEM((inst, 2, I_LO, tt), F32),
                        pltpu.VMEM((tt, I_LO, 2 * inst), F32)],
        compiler_params=_params(),
        name="wkv_scan",
    )(a, w, b, k, r, v, s0)


def _wkv_step_kernel(a_ref, w_ref, b_ref, k_ref, r_ref, v_ref, s_ref, y_ref, so_ref):
    s = s_ref[...]
    row = lax.broadcasted_iota(jnp.int32, (HEAD, HEAD), 0)
    col = lax.broadcasted_iota(jnp.int32, (HEAD, HEAD), 1)
    eye = (row == col).astype(F32)
    sa = jnp.sum(s * a_ref[...], axis=-1, keepdims=True)
    v_col = jnp.sum(eye * v_ref[...], axis=-1, keepdims=True)
    s_new = s * w_ref[...] + sa * b_ref[...] + v_col * k_ref[...]
    so_ref[...] = s_new
    y_col = jnp.sum(s_new * r_ref[...], axis=-1, keepdims=True)
    y_ref[...] = jnp.sum(eye * y_col, axis=-2, keepdims=True)


def _wkv_step_call(a, w, b, k, r, v, s, bb):
    bsz = s.shape[0]
    op = pl.BlockSpec((bb, N_HEADS, 1, HEAD), lambda c: (c, 0, 0, 0))
    st = pl.BlockSpec((bb, N_HEADS, HEAD, HEAD), lambda c: (c, 0, 0, 0))
    return pl.pallas_call(
        _wkv_step_kernel,
        grid=(bsz // bb,),
        in_specs=[op] * 6 + [st],
        out_specs=[op, st],
        out_shape=[jax.ShapeDtypeStruct((bsz, N_HEADS, 1, HEAD), F32),
                   jax.ShapeDtypeStruct((bsz, N_HEADS, HEAD, HEAD), F32)],
        compiler_params=_params(),
        name="wkv_step",
    )(a, w, b, k, r, v, s)


def _route(logits):
    lane = lax.broadcasted_iota(jnp.int32, logits.shape, 1)
    neg = jnp.float32(-jnp.inf)
    big = jnp.int32(ROUTER_LANES)
    is_grp = (lane >= N_EXPERTS) & (lane < N_EXPERTS + N_GROUPS)
    gl = jnp.where(is_grp, logits, neg)
    gmax = jnp.max(gl, axis=-1, keepdims=True)
    ge = jnp.where(is_grp, jnp.exp(gl - gmax), 0.0)
    gp = ge / jnp.sum(ge, axis=-1, keepdims=True)
    gi = jnp.min(jnp.where(gl == gmax, lane, big), axis=-1, keepdims=True)
    p_grp = jnp.sum(jnp.where(lane == gi, gp, 0.0), axis=-1, keepdims=True)
    gidx = gi - N_EXPERTS
    in_grp = (lane >= gidx * EXP_PER_GROUP) & (lane < (gidx + 1) * EXP_PER_GROUP)
    el = jnp.where(in_grp, logits, neg)
    emax = jnp.max(el, axis=-1, keepdims=True)
    ee = jnp.where(in_grp, jnp.exp(el - emax), 0.0)
    pe = ee / jnp.sum(ee, axis=-1, keepdims=True)
    pe_m = jnp.where(in_grp, pe, -1.0)
    v1 = jnp.max(pe_m, axis=-1, keepdims=True)
    i1 = jnp.min(jnp.where(pe_m == v1, lane, big), axis=-1, keepdims=True)
    pe_m2 = jnp.where(lane == i1, -1.0, pe_m)
    v2 = jnp.max(pe_m2, axis=-1, keepdims=True)
    i2 = jnp.min(jnp.where(pe_m2 == v2, lane, big), axis=-1, keepdims=True)
    tot = v1 + v2
    return jnp.where(lane == i1, v1 / tot * p_grp, 0.0) + jnp.where(lane == i2, v2 / tot * p_grp, 0.0)


def _mix_post_kernel(precise, batch_major, tt, bsz,
                     ya_ref, bonus_ref, g_ref, yb_ref, x_ref, g1_ref, sh2_ref, sc2_ref, wout_ref, *refs):
    wout_lo = refs[0] if precise else None
    (lnxg_ref, lnxb_ref, gones_ref, ln1g_ref, ln1b_ref, wrh_ref, wrl_ref, br_ref,
     x1_o, h2_o, comb_o, ya_tm_s, *tm_s) = refs[1:] if precise else refs

    def wout(lo_hi_ref, r0, r1):
        return None if lo_hi_ref is None else lo_hi_ref[r0:r1, :]

    def mix(t0, nt):
        nrows = nt * bsz
        steps = pl.ds(t0, nt)

        def gsum(z):
            return _group_sum(z.reshape(nrows, C_A), gones_ref[...], precise).reshape(nt, bsz, C_A)

        ya = _load_batch_major(ya_ref, ya_tm_s, t0, nt)
        mu = gsum(ya) * (1.0 / HEAD)
        yc = ya - mu
        var = gsum(yc * yc) * (1.0 / HEAD)
        yn = yc * lax.rsqrt(var + GN_EPS) * lnxg_ref[...] + lnxb_ref[...]
        ya = (yn + bonus_ref[steps]) * g_ref[steps]
        return (_mm(_split(ya.reshape(nrows, C_A), precise), wout_ref[0:C_A, :], wout(wout_lo, 0, C_A))
                + _mm(_split(yb_ref[steps].reshape(nrows, C_B), precise), wout_ref[C_A:, :],
                      wout(wout_lo, C_A, C_A + C_B))).reshape(nt, bsz, D_MODEL)

    def post(t0, nt, y):
        nrows = nt * bsz
        steps = pl.ds(t0, nt)
        out_rows = pl.ds(t0 * bsz, nrows)
        x = _load_batch_major(x_ref, tm_s[0], t0, nt) if batch_major else x_ref[steps]
        x1 = _layer_norm(ALPHA * x + (1.0 + g1_ref[...]) * y) * ln1g_ref[...] + ln1b_ref[...]
        x1_o[steps] = x1
        h2 = (_layer_norm(x1) * (1.0 + sc2_ref[...]) + sh2_ref[...]).reshape(nrows, D_MODEL)
        hi = h2.astype(BF16)
        lo = (h2 - hi.astype(F32)).astype(BF16)
        h2_o[out_rows, :] = h2 if h2_o.dtype == F32 else hi
        logits = _dot(hi, wrh_ref[...]) + _dot(lo, wrh_ref[...]) + _dot(hi, wrl_ref[...]) + br_ref[...]
        comb_o[out_rows, :] = _route(logits)

    n_parts = MIX_PARTS if tt % MIX_PARTS == 0 else 1
    nt = tt // n_parts
    y = mix(0, nt)
    for i in range(n_parts):
        y_next = mix((i + 1) * nt, nt) if i + 1 < n_parts else None
        post(i * nt, nt, y)
        y = y_next


def _mix_post_call(ya, bonus, g, yb, x, batch_major, tt, g1, sh2, sc2, wl):
    bsz, t_len, _ = ya.shape
    rows = tt * bsz
    ya_spec = pl.BlockSpec((bsz, tt, C_A), lambda c: (0, c, 0))
    if batch_major:
        x_spec = pl.BlockSpec((bsz, tt, D_MODEL), lambda c: (0, c, 0))
    else:
        x_spec = pl.BlockSpec((tt, bsz, D_MODEL), lambda c: (c, 0, 0))
    seq = pl.BlockSpec((tt, bsz, C_A), lambda c: (c, 0, 0))
    precise = wl['precise']
    weights = [wl['w_out']] + ([wl['w_out_lo']] if precise else []) + [
        wl['lnx_gain'], wl['lnx_bias'], wl['gones'], wl['ln1_gain'], wl['ln1_bias'],
        wl['w_router_hi'], wl['w_router_lo'], wl['b_router']]
    scratch = [pltpu.VMEM((tt, bsz, C_A), F32)] + ([pltpu.VMEM((tt, bsz, D_MODEL), F32)] if batch_major else [])
    return pl.pallas_call(
        functools.partial(_mix_post_kernel, precise, batch_major, tt, bsz),
        grid=(t_len // tt,),
        in_specs=[ya_spec] + [seq] * 3 + [x_spec] + [_full((bsz, D_MODEL))] * 3 + [_const(w.shape) for w in weights],
        out_specs=[pl.BlockSpec((tt, bsz, D_MODEL), lambda c: (c, 0, 0)),
                   pl.BlockSpec((rows, D_MODEL), lambda c: (c, 0)),
                   pl.BlockSpec((rows, ROUTER_LANES), lambda c: (c, 0))],
        out_shape=[jax.ShapeDtypeStruct((t_len, bsz, D_MODEL), F32),
                   jax.ShapeDtypeStruct((t_len * bsz, D_MODEL), F32 if wl['moe_precise'] else BF16),
                   jax.ShapeDtypeStruct((t_len * bsz, ROUTER_LANES), F32)],
        scratch_shapes=scratch,
        compiler_params=_params(),
        name="mix_post",
    )(ya, bonus, g, yb, x, g1, sh2, sc2, *weights)


def _moe_kernel(precise, batch_major_out, tt, bsz, h2_ref, comb_ref, x1_ref, g2_ref, *refs):
    wg_ref, wu_ref, wd_ref, ln2g_ref, ln2b_ref, o_ref, acc_s = refs
    weight = lambda r, i: _split(r[i], precise)
    step = pl.program_id(1)
    rows = tt * bsz

    @pl.when(step == 0)
    def _():
        acc_s[...] = jnp.zeros_like(acc_s)

    h = _split(h2_ref[...], True) if precise else (h2_ref[...], None)
    lane = lax.broadcasted_iota(jnp.int32, (rows, ROUTER_LANES), 1)
    comb = comb_ref[...]
    down = None
    for i in range(MOE_EXPERTS_PER_STEP):
        ce = jnp.sum(jnp.where(lane == step * MOE_EXPERTS_PER_STEP + i, comb, 0.0), axis=-1, keepdims=True)
        gate = _mm(h, *weight(wg_ref, i))
        hid = (gate * _sigmoid(gate)) * _mm(h, *weight(wu_ref, i))
        part = _mm(_split(hid * ce, precise), *weight(wd_ref, i))
        down = part if down is None else down + part
    acc_s[...] += down

    @pl.when(step == N_EXPERTS // MOE_EXPERTS_PER_STEP - 1)
    def _():
        moe = acc_s[...].reshape(tt, bsz, D_MODEL)
        out = _layer_norm(ALPHA * x1_ref[...] + (1.0 + g2_ref[...]) * moe) * ln2g_ref[...] + ln2b_ref[...]
        if batch_major_out:
            _store_batch_major(o_ref, out, 0, tt)
        else:
            o_ref[...] = out


def _moe_call(h2, comb, x1, batch_major_out, tt, g2, wl):
    precise = wl['moe_precise']
    t_len, bsz, _ = x1.shape
    rows = tt * bsz
    first = wl['layer'] * (N_EXPERTS // MOE_EXPERTS_PER_STEP)
    w_gate = pl.BlockSpec((MOE_EXPERTS_PER_STEP, D_MODEL, D_EXPERT), lambda c, e: (first + e, 0, 0))
    w_down = pl.BlockSpec((MOE_EXPERTS_PER_STEP, D_EXPERT, D_MODEL), lambda c, e: (first + e, 0, 0))
    weights = [wl['w_exp_gate'], wl['w_exp_up'], wl['w_exp_down']]
    w_specs = [w_gate, w_gate, w_down]
    if batch_major_out:
        o_spec = pl.BlockSpec((bsz, tt, D_MODEL), lambda c, e: (0, c, 0))
        o_shape = jax.ShapeDtypeStruct((bsz, t_len, D_MODEL), F32)
    else:
        o_spec = pl.BlockSpec((tt, bsz, D_MODEL), lambda c, e: (c, 0, 0))
        o_shape = jax.ShapeDtypeStruct((t_len, bsz, D_MODEL), F32)
    return pl.pallas_call(
        functools.partial(_moe_kernel, precise, batch_major_out, tt, bsz),
        grid=(t_len // tt, N_EXPERTS // MOE_EXPERTS_PER_STEP),
        in_specs=[pl.BlockSpec((rows, D_MODEL), lambda c, e: (c, 0)),
                  pl.BlockSpec((rows, ROUTER_LANES), lambda c, e: (c, 0)),
                  pl.BlockSpec((tt, bsz, D_MODEL), lambda c, e: (c, 0, 0)),
                  pl.BlockSpec((bsz, D_MODEL), lambda c, e: (0, 0))] + w_specs + [
                  pl.BlockSpec((1, D_MODEL), lambda c, e: (0, 0)),
                  pl.BlockSpec((1, D_MODEL), lambda c, e: (0, 0))],
        out_specs=o_spec,
        out_shape=o_shape,
        scratch_shapes=[pltpu.VMEM((rows, D_MODEL), F32)],
        compiler_params=_params(2, MOE_VMEM_LIMIT),
        name="moe",
    )(h2, comb, x1, g2, *weights, wl['ln2_gain'], wl['ln2_bias'])


def _pad_cols(w, width):
    return jnp.pad(w, ((0, 0), (0, width - w.shape[1])))


def _pad_rows(w, height):
    return jnp.pad(w, ((0, height - w.shape[0]), (0, 0)))


def _split_weight_kernel(w_ref, hi_ref, lo_ref):
    hi_ref[...], lo_ref[...] = _split(w_ref[...])


def _split_weight(w):
    w2 = w.reshape(-1, w.shape[-1])
    rows, cols = w2.shape
    tr = min(rows, 512)
    spec = pl.BlockSpec((tr, cols), lambda i: (i, 0))
    hi, lo = pl.pallas_call(
        _split_weight_kernel,
        grid=(rows // tr,),
        in_specs=[spec],
        out_specs=[spec, spec],
        out_shape=[jax.ShapeDtypeStruct(w2.shape, BF16)] * 2,
        compiler_params=_params(),
        name="split_weight",
    )(w2)
    return hi.reshape(w.shape), lo.reshape(w.shape)


def _block_diag(w):
    n, c, d = w.shape
    eye = jnp.eye(n, dtype=w.dtype)
    return (eye[:, None, :, None] * w[:, :, None, :]).reshape(n * c, n * d)


def _prep_layer(p, l, precise, moe_precise):
    row = lambda v: v[l].reshape(1, -1)
    w_in = p['w_in'][l]
    o1, o2, o3 = 3 * C_A, 3 * C_A + LORA_DECAY, 3 * C_A + LORA_DECAY + LORA_AAA
    pieces = [(0, o1, o1), (o1, o2, LORA_PAD), (o2, o3, LORA_PAD), (o3, N_SHIFT, LORA_PAD)]
    w_in_p = jnp.concatenate([_pad_cols(w_in[:, a:b], wd) for a, b, wd in pieces] + [w_in[:, N_SHIFT:]], axis=1)
    mu = p['mu_shift'][l].reshape(1, -1)
    mu_p = jnp.concatenate([_pad_cols(mu[:, a:b], wd) for a, b, wd in pieces], axis=1)
    w_router = jnp.concatenate([p['w_router_expert'][l], p['w_router_group'][l]], axis=1)
    w_router = _pad_cols(w_router, ROUTER_LANES)
    w_router_hi, w_router_lo = _split_weight(w_router)
    b_router = _pad_cols(jnp.concatenate([p['b_router_expert'][l], p['b_router_group'][l]]).reshape(1, -1),
                         ROUTER_LANES)
    head_id = jnp.arange(C_A) // HEAD
    mats = dict(
        w_in=w_in_p,
        w_decay_up=_pad_rows(p['w_decay_up'][l], LORA_PAD),
        w_a_up=_pad_rows(p['w_a_up'][l], LORA_PAD),
        w_g_up=_pad_rows(p['w_g_up'][l], LORA_PAD),
        w_gates=jnp.concatenate([_block_diag(p['w_rgate'][l]), _block_diag(p['w_igate'][l])], axis=1),
        w_out=p['w_out'][l])
    split_mats = {}
    for name, w in mats.items():
        if precise:
            split_mats[name], split_mats[name + '_lo'] = _split_weight(w)
        else:
            split_mats[name] = w.astype(BF16)
    for name in ('w_exp_gate', 'w_exp_up', 'w_exp_down'):
        split_mats[name] = p[name].reshape((-1,) + p[name].shape[2:])
    return dict(
        split_mats, layer=l, precise=precise, moe_precise=moe_precise, mu=mu_p, w0=row(p['w0']), a0=row(p['a0']),
        k_k=row(p['k_k']), k_a=row(p['k_a']), r_k=row(p['r_k']),
        conv_w=p['conv_w'][l], conv_b=row(p['conv_b']),
        b_gates=jnp.concatenate([p['b_rgate'][l], p['b_igate'][l]]).reshape(1, -1),
        lam=row(p['lru_lambda']),
        gones=(head_id[:, None] == head_id[None, :]).astype(BF16),
        lnx_gain=row(p['lnx_gain']), lnx_bias=row(p['lnx_bias']),
        ln1_gain=row(p['ln1_gain']), ln1_bias=row(p['ln1_bias']),
        w_router_hi=w_router_hi, w_router_lo=w_router_lo,
        b_router=b_router,
        ln2_gain=row(p['ln2_gain']), ln2_bias=row(p['ln2_bias']),
    )


def _trunk(x, batch_major, tt, mods, st_wkv, st_shift, st_conv, st_lru, layers, moe_tt, scan_tt):
    wkv_out, shift_out, conv_out, lru_out = [], [], [], []
    for l in range(DEPTH):
        wl = layers[l]
        sh1, sc1, g1, sh2, sc2, g2 = mods[l]
        bm_in = batch_major and l == 0
        conv_state = jnp.swapaxes(st_conv[l], 0, 1)
        (r, w, k, v, a, b, g, bonus, yb, shift_new, conv_new, lru_new) = _mix_pre_call(
            x, bm_in, tt, sh1, sc1, st_shift[l], conv_state, st_lru[l], wl)
        bsz, t_len, _ = r.shape
        if t_len > 1:
            s0 = st_wkv[l].reshape(bsz, N_HEADS, 2, I_LO, HEAD).transpose(4, 3, 2, 0, 1)
            s0 = s0.reshape(HEAD, I_LO, 2 * HEAD)
            ya, s_fin = _wkv_scan_call(a, w, b, k, r, v, s0, scan_tt)
            s_new = s_fin.reshape(HEAD, I_LO, 2, bsz, N_HEADS).transpose(3, 4, 2, 1, 0)
            s_new = s_new.reshape(bsz, N_HEADS, HEAD, HEAD)
        else:
            shp = (bsz, N_HEADS, 1, HEAD)
            y4, s_new = _wkv_step_call(a.reshape(shp), w.reshape(shp), b.reshape(shp), k.reshape(shp),
                                       r.reshape(shp), v.reshape(shp), st_wkv[l], WKV_STEP_BATCH)
            ya = y4.reshape(bsz, 1, C_A)
        x1, h2, comb = _mix_post_call(ya, bonus, g, yb, x, bm_in, tt, g1, sh2, sc2, wl)
        bm_out = batch_major and l == DEPTH - 1
        x = _moe_call(h2, comb, x1, bm_out, moe_tt, g2, wl)
        wkv_out.append(s_new)
        shift_out.append(shift_new)
        conv_out.append(jnp.swapaxes(conv_new, 0, 1))
        lru_out.append(lru_new)
    return x, jnp.stack(wkv_out), jnp.stack(shift_out), jnp.stack(conv_out), jnp.stack(lru_out)


def kernel(x_prompt, x_sample, c_prompt, c_sample, state_wkv, state_shift, state_conv, state_lru, w_ada, b_ada, w_in, mu_shift, w0, w_decay_up, a0, w_a_up, w_g_up, k_k, k_a, r_k, lnx_gain, lnx_bias, conv_w, conv_b, w_rgate, b_rgate, w_igate, b_igate, lru_lambda, w_out, ln1_gain, ln1_bias, w_router_group, b_router_group, w_router_expert, b_router_expert, w_exp_gate, w_exp_up, w_exp_down, ln2_gain, ln2_bias):
    p = dict(w_in=w_in, mu_shift=mu_shift, w0=w0, w_decay_up=w_decay_up, a0=a0, w_a_up=w_a_up,
             w_g_up=w_g_up, k_k=k_k, k_a=k_a, r_k=r_k.reshape(DEPTH, C_A), lnx_gain=lnx_gain,
             lnx_bias=lnx_bias, conv_w=conv_w, conv_b=conv_b, w_rgate=w_rgate, b_rgate=b_rgate,
             w_igate=w_igate, b_igate=b_igate, lru_lambda=lru_lambda, w_out=w_out, ln1_gain=ln1_gain,
             ln1_bias=ln1_bias, w_router_group=w_router_group, b_router_group=b_router_group,
             w_router_expert=w_router_expert, b_router_expert=b_router_expert, w_exp_gate=w_exp_gate,
             w_exp_up=w_exp_up, w_exp_down=w_exp_down, ln2_gain=ln2_gain, ln2_bias=ln2_bias)
    prepared = [_prep_layer(p, l, precise=True, moe_precise=(l < DEPTH - 1)) for l in range(DEPTH)]
    layers_s = prepared
    layers_p = [dict(prepared[l], precise=(l == 0), moe_precise=False) for l in range(DEPTH)]
    bp, bs = x_prompt.shape[0], x_sample.shape[0]

    mod = _ada_call(jnp.concatenate([c_sample, c_prompt], axis=0), w_ada, b_ada)
    split = lambda m: [m[:, i * D_MODEL:(i + 1) * D_MODEL] for i in range(6)]
    mods_s = [split(mod[l, :bs]) for l in range(DEPTH)]
    mods_p = [split(mod[l, bs:bs + bp]) for l in range(DEPTH)]

    z_wkv = jnp.zeros((DEPTH, bp, N_HEADS, HEAD, HEAD), F32)
    z_shift = jnp.zeros((DEPTH, bp, D_MODEL), F32)
    z_conv = jnp.zeros((DEPTH, bp, CONV_W - 1, C_B), F32)
    z_lru = jnp.zeros((DEPTH, bp, C_B), F32)
    y_p, wkv_p, shift_p, conv_p, lru_p = _trunk(x_prompt, True, MIX_TT, mods_p, z_wkv, z_shift, z_conv, z_lru,
                                                layers_p, MOE_TT, SCAN_TT)
    xs_tm = x_sample.reshape(1, bs, D_MODEL)
    y_s, wkv_s, shift_s, conv_s, lru_s = _trunk(xs_tm, False, 1, mods_s, state_wkv, state_shift, state_conv,
                                                state_lru, layers_s, 1, 1)
    return (y_p, y_s.reshape(bs, 1, D_MODEL), wkv_p, shift_p, conv_p, lru_p, wkv_s, shift_s, conv_s, lru_s)
```

```python
import functools

import jax
import jax.numpy as jnp
from jax import lax
from jax.experimental import pallas as pl
from jax.experimental.pallas import tpu as pltpu

F32 = jnp.float32
BF16 = jnp.bfloat16

D_MODEL = 1024
DEPTH = 2
C_A = 512
C_B = 512
HEAD = 64
N_HEADS = C_A // HEAD
LORA_DECAY = 32
LORA_AAA = 32
LORA_GATE = 96
LORA_PAD = 128
N_SHIFT = 3 * C_A + LORA_DECAY + LORA_AAA + LORA_GATE
N_SHIFT_PAD = 3 * C_A + 3 * LORA_PAD
N_IN_PAD = N_SHIFT_PAD + 2 * C_B
CONV_W = 4
LRU_BLOCKS = 8
LRU_C = 8.0
N_GROUPS = 4
EXP_PER_GROUP = 4
N_EXPERTS = 16
MOE_EXPERTS_PER_STEP = 4
D_EXPERT = 256
ROUTER_LANES = 128
ALPHA = (2 * DEPTH) ** 0.25
LN_EPS = 1e-5
GN_EPS = 64e-5
VMEM_LIMIT = 56 * 1024 * 1024
MOE_VMEM_LIMIT = 62 * 1024 * 1024
MIX_TT = 64
POST_TT = 128
MOE_TT = 128
SCAN_TT = 128
WKV_STEP_BATCH = 16


def _params(n_axes=1, vmem_limit=VMEM_LIMIT):
    return pltpu.CompilerParams(dimension_semantics=("arbitrary",) * n_axes,
                                vmem_limit_bytes=vmem_limit)


def _full(shape):
    return pl.BlockSpec(shape, lambda *_: (0,) * len(shape))


def _const(shape):
    return pl.BlockSpec(shape, lambda *_: (0,) * len(shape), pipeline_mode=pl.Buffered(1))


def _layer_norm(x):
    mu = jnp.mean(x, axis=-1, keepdims=True)
    xc = x - mu
    var = jnp.mean(xc * xc, axis=-1, keepdims=True)
    return xc * lax.rsqrt(var + LN_EPS)


def _softplus(z):
    return jnp.maximum(z, 0.0) + jnp.log1p(jnp.exp(-jnp.abs(z)))


def _sigmoid(z):
    return 1.0 / (1.0 + jnp.exp(-z))


def _gelu_tanh(x):
    c = 0.7978845608028654
    return x * (0.5 * (1.0 + jnp.tanh(c * (x + 0.044715 * (x * x * x)))))


def _dot(a, b):
    return jnp.dot(a, b, preferred_element_type=F32)


def _split(x, precise=True):
    hi = x.astype(BF16)
    return hi, ((x - hi.astype(F32)).astype(BF16) if precise else None)


def _mm(xs, w_hi, w_lo=None):
    hi, lo = xs
    out = _dot(hi, w_hi)
    if w_lo is not None:
        out = out + _dot(lo, w_hi) + _dot(hi, w_lo)
    return out


def _group_sum(x, ones_bf16, precise):
    hi, lo = _split(x, precise)
    out = _dot(hi, ones_bf16)
    return out + _dot(lo, ones_bf16) if precise else out


def _load_batch_major(x_ref, tm_ref, t0, nt):
    for t in range(t0, t0 + nt):
        tm_ref[t] = x_ref[:, t, :]
    return tm_ref[pl.ds(t0, nt)]


def _store_batch_major(o_ref, val, t0, nt):
    for t in range(nt):
        o_ref[:, t0 + t, :] = val[t]


def _ada_kernel(c_ref, w_ref, b_ref, o_ref):
    c = c_ref[...]
    w_hi, w_lo = _split(w_ref[0])
    o_ref[0] = _mm(_split(c * _sigmoid(c)), w_hi, w_lo) + b_ref[0]


def _ada_call(c_all, w_ada, b_ada):
    n = c_all.shape[0]
    tn = 1536
    return pl.pallas_call(
        _ada_kernel,
        grid=(DEPTH, 6 * D_MODEL // tn),
        in_specs=[pl.BlockSpec((n, D_MODEL), lambda l, j: (0, 0)),
                  pl.BlockSpec((1, D_MODEL, tn), lambda l, j: (l, 0, j)),
                  pl.BlockSpec((1, 1, tn), lambda l, j: (l, 0, j))],
        out_specs=pl.BlockSpec((1, n, tn), lambda l, j: (l, 0, j)),
        out_shape=jax.ShapeDtypeStruct((DEPTH, n, 6 * D_MODEL), F32),
        compiler_params=_params(2),
        name="ada_mod",
    )(c_all, w_ada, b_ada.reshape(DEPTH, 1, 6 * D_MODEL))


MIX_PARTS = 2
N_PRE_MATS = 5


def _mix_pre_kernel(precise, batch_major, tt, bsz,
                    x_ref, sh_ref, sc_ref, hprev_ref, cst_ref, lst_ref, *refs):
    n_mat = N_PRE_MATS * (2 if precise else 1)
    mats = refs[:n_mat]
    if precise:
        (win_ref, win_lo), (wdu_ref, wdu_lo), (wau_ref, wau_lo), (wgu_ref, wgu_lo), (wg_ref, wg_lo) = (
            (mats[2 * i], mats[2 * i + 1]) for i in range(N_PRE_MATS))
    else:
        (win_ref, win_lo), (wdu_ref, wdu_lo), (wau_ref, wau_lo), (wgu_ref, wgu_lo), (wg_ref, wg_lo) = (
            (m, None) for m in mats)
    (mu_ref, w0_ref, a0_ref, kk_ref, ka_ref, rk_ref, cw_ref, cb_ref, bg_ref, lam_ref, gones_ref,
     r_o, w_o, k_o, v_o, a_o, b_o, g_o, bonus_o, yb_o, shift_o, conv_o, lru_o,
     prevp_s, conv_s, lru_s, a_s, u_s, hs_s, *tm_s) = refs[n_mat:]

    def win(lo_hi_ref, c0, c1):
        return None if lo_hi_ref is None else lo_hi_ref[:, c0:c1]

    @pl.when(pl.program_id(0) == 0)
    def _():
        prevp_s[...] = _mm(_split(hprev_ref[...], precise), win_ref[:, :N_SHIFT_PAD], win(win_lo, 0, N_SHIFT_PAD))
        conv_s[...] = cst_ref[...]
        lru_s[...] = lst_ref[...]

    col_blocks = ((0, C_A), (C_A, C_A), (2 * C_A, C_A), (3 * C_A, 3 * LORA_PAD),
                  (N_SHIFT_PAD, C_B), (N_SHIFT_PAD + C_B, C_B))

    def project(t0, nt):
        nrows = nt * bsz
        x = _load_batch_major(x_ref, tm_s[0], t0, nt) if batch_major else x_ref[pl.ds(t0, nt)]
        h = _layer_norm(x) * (1.0 + sc_ref[...]) + sh_ref[...]
        if t0 + nt == tt:
            shift_o[...] = h[nt - 1]
        hb = _split(h.reshape(nrows, D_MODEL), precise)
        return [_mm(hb, win_ref[:, off:off + width], win(win_lo, off, off + width)).reshape(nt, bsz, width)
                for off, width in col_blocks]

    def finish(t0, nt, proj):
        nrows = nt * bsz
        steps = pl.ds(t0, nt)

        def shifted(idx):
            off, width = col_blocks[idx]
            p = proj[idx]
            first = prevp_s[:, off:off + width][None]
            prev = jnp.concatenate([first, p[:nt - 1]], axis=0) if nt > 1 else first
            prevp_s[:, off:off + width] = p[nt - 1]
            return p + (prev - p) * mu_ref[:, off:off + width]

        def up(z, w_ref_, w_lo_):
            zs = _split(z.reshape(nrows, LORA_PAD), precise)
            return _mm(zs, w_ref_[...], None if w_lo_ is None else w_lo_[...]).reshape(nt, bsz, C_A)

        def gsum(z):
            return _group_sum(z.reshape(nrows, C_A), gones_ref[...], precise).reshape(nt, bsz, C_A)

        r, k, v, lora = (shifted(idx) for idx in range(4))
        gate_b, rec_b = proj[4], proj[5]
        wd = lora[:, :, 0:LORA_PAD]
        ad = lora[:, :, LORA_PAD:2 * LORA_PAD]
        gd = lora[:, :, 2 * LORA_PAD:3 * LORA_PAD]

        w_log = -_softplus(-(w0_ref[...] + up(jnp.tanh(wd), wdu_ref, wdu_lo))) - 0.5
        decay = jnp.exp(-jnp.exp(w_log))
        a_sig = _sigmoid(a0_ref[...] + up(ad, wau_ref, wau_lo))
        g_o[steps] = up(_sigmoid(gd), wgu_ref, wgu_lo)

        kk = k * kk_ref[...]
        kk = kk / jnp.maximum(jnp.sqrt(gsum(kk * kk)), 1e-12)
        k2 = k * (1.0 + (a_sig - 1.0) * ka_ref[...])
        for o_ref, val in ((r_o, r), (w_o, decay), (k_o, k2), (v_o, v), (a_o, -kk), (b_o, kk * a_sig)):
            _store_batch_major(o_ref, val, t0, nt)
        bonus_o[steps] = gsum(r * k2 * rk_ref[...]) * v

        xp = jnp.concatenate([conv_s[...], rec_b], axis=0)
        xc = cb_ref[...] + xp[0:nt] * cw_ref[0:1, :]
        for j in range(1, CONV_W):
            xc = xc + xp[j:j + nt] * cw_ref[j:j + 1, :]
        conv_s[...] = xp[nt:nt + CONV_W - 1]

        gates = _mm(_split(xc.reshape(nrows, C_B), precise), wg_ref[...],
                    None if wg_lo is None else wg_lo[...]) + bg_ref[...]
        r_t = _sigmoid(gates[:, :C_B]).reshape(nt, bsz, C_B)
        i_t = _sigmoid(gates[:, C_B:]).reshape(nt, bsz, C_B)
        log_a = -LRU_C * r_t * _softplus(-lam_ref[...])
        a_s[steps] = jnp.exp(log_a)
        u_s[steps] = jnp.sqrt(1.0 - jnp.exp(2.0 * log_a)) * (i_t * xc)
        yb_o[steps] = _gelu_tanh(gate_b)

    n_parts = MIX_PARTS if tt % MIX_PARTS == 0 else 1
    nt = tt // n_parts
    proj = project(0, nt)
    for i in range(n_parts):
        proj_next = project((i + 1) * nt, nt) if i + 1 < n_parts else None
        finish(i * nt, nt, proj)
        proj = proj_next
    conv_o[...] = conv_s[...]

    def scan_step(t, hcur):
        hcur = a_s[t] * hcur + u_s[t]
        hs_s[t] = hcur
        return hcur

    h_fin = lax.fori_loop(0, tt, scan_step, lru_s[...])
    lru_s[...] = h_fin
    lru_o[...] = h_fin
    yb_o[...] = hs_s[...] * yb_o[...]


def _mix_pre_call(x, batch_major, tt, sh1, sc1, h_prev, conv_state, lru_state, wl):
    precise = wl['precise']
    if batch_major:
        bsz, t_len, _ = x.shape
        x_spec = pl.BlockSpec((bsz, tt, D_MODEL), lambda c: (0, c, 0))
    else:
        t_len, bsz, _ = x.shape
        x_spec = pl.BlockSpec((tt, bsz, D_MODEL), lambda c: (c, 0, 0))
    seq = lambda ch: pl.BlockSpec((tt, bsz, ch), lambda c: (c, 0, 0))
    weights = []
    for name in ('w_in', 'w_decay_up', 'w_a_up', 'w_g_up', 'w_gates'):
        weights.append(wl[name])
        if precise:
            weights.append(wl[name + '_lo'])
    weights += [wl['mu'], wl['w0'], wl['a0'], wl['k_k'], wl['k_a'], wl['r_k'], wl['conv_w'], wl['conv_b'],
                wl['b_gates'], wl['lam'], wl['gones']]
    seq_out = jax.ShapeDtypeStruct((t_len, bsz, C_A), F32)
    scan_spec = pl.BlockSpec((bsz, tt, C_A), lambda c: (0, c, 0))
    scan_out = jax.ShapeDtypeStruct((bsz, t_len, C_A), F32)
    scratch = [pltpu.VMEM((bsz, N_SHIFT_PAD), F32), pltpu.VMEM((CONV_W - 1, bsz, C_B), F32),
               pltpu.VMEM((bsz, C_B), F32), pltpu.VMEM((tt, bsz, C_B), F32),
               pltpu.VMEM((tt, bsz, C_B), F32), pltpu.VMEM((tt, bsz, C_B), F32)]
    if batch_major:
        scratch.append(pltpu.VMEM((tt, bsz, D_MODEL), F32))
    return pl.pallas_call(
        functools.partial(_mix_pre_kernel, precise, batch_major, tt, bsz),
        grid=(t_len // tt,),
        in_specs=[x_spec, _full((bsz, D_MODEL)), _full((bsz, D_MODEL)), _full((bsz, D_MODEL)),
                  _full((CONV_W - 1, bsz, C_B)), _full((bsz, C_B))] + [_const(w.shape) for w in weights],
        out_specs=[scan_spec] * 6 + [seq(C_A)] * 3 + [_full((bsz, D_MODEL)), _full((CONV_W - 1, bsz, C_B)),
                                                      _full((bsz, C_B))],
        out_shape=[scan_out] * 6 + [seq_out] * 3 + [jax.ShapeDtypeStruct((bsz, D_MODEL), F32),
                                   jax.ShapeDtypeStruct((CONV_W - 1, bsz, C_B), F32),
                                   jax.ShapeDtypeStruct((bsz, C_B), F32)],
        scratch_shapes=scratch,
        compiler_params=_params(),
        name="mix_pre",
    )(x, sh1, sc1, h_prev, conv_state, lru_state, *weights)


I_LO = HEAD // 2
SUB = 8
N_STRIP = I_LO // SUB


def _strips(ref, *lead):
    return [ref[(*lead, pl.ds(q * SUB, SUB), slice(None))] for q in range(N_STRIP)]


def _row(ref, *idx):
    return jnp.broadcast_to(ref[(*idx, slice(None))], (SUB, 2 * HEAD))


A_OP, W_OP, B_OP, K_OP, R_OP = range(5)
RELAYOUT_UNROLL = 16


V_RAW = 5


def _wkv_scan_kernel(tt, a_hbm, w_hbm, b_hbm, k_hbm, r_hbm, v_hbm, s0_ref,
                     y_ref, sfin_ref, s_s, raw_s, raw_sem, ops_s, zt_s, vop_s, vt_s, ysc_s):
    hbm = (a_hbm, w_hbm, b_hbm, k_hbm, r_hbm, v_hbm)
    bsz = raw_s.shape[1]
    inst = bsz * N_HEADS
    chunk = pl.program_id(0)

    def raw_copy(n, c):
        t0 = pl.multiple_of(c * tt, tt)
        return pltpu.make_async_copy(hbm[n].at[:, pl.ds(t0, tt), :], raw_s.at[n], raw_sem.at[n])

    @pl.when(chunk == 0)
    def _():
        for n in range(len(hbm)):
            raw_copy(n, 0).start()
        s_s[...] = s0_ref[...]

    for n in range(V_RAW):
        raw_copy(n, chunk).wait()
        for bi in range(bsz):
            zt_s[bi * N_HEADS:(bi + 1) * N_HEADS] = raw_s[n, bi].T.reshape(N_HEADS, HEAD, tt)

        def key_rows(j, carry, n=n):
            rows = zt_s[:, j, :]
            ops_s[n, j] = jnp.concatenate([rows, rows], axis=0).T
            return carry

        lax.fori_loop(0, HEAD, key_rows, 0, unroll=RELAYOUT_UNROLL)

    raw_copy(V_RAW, chunk).wait()
    for bi in range(bsz):
        vt_s[bi * N_HEADS:(bi + 1) * N_HEADS] = raw_s[V_RAW, bi].T.reshape(N_HEADS, 2, I_LO, tt)

    def value_rows(il, carry):
        vop_s[:, il, :] = jnp.concatenate([vt_s[:, 0, il, :], vt_s[:, 1, il, :]], axis=0).T
        return carry

    lax.fori_loop(0, I_LO, value_rows, 0, unroll=RELAYOUT_UNROLL)

    @pl.when(chunk + 1 < pl.num_programs(0))
    def _():
        for n in range(len(hbm)):
            raw_copy(n, chunk + 1).start()

    sa = [None] * N_STRIP
    for j in range(HEAD):
        a8 = _row(ops_s, A_OP, j, pl.ds(0, 1))
        for q, s in enumerate(_strips(s_s, j)):
            sa[q] = s * a8 if sa[q] is None else sa[q] + s * a8

    def step(t, sa):
        t_next = jnp.minimum(t + 1, tt - 1)
        v = _strips(vop_s, t)
        y = [None] * N_STRIP
        sa_next = [None] * N_STRIP
        for j in range(HEAD):
            w8, b8, k8, r8 = (_row(ops_s, n, j, pl.ds(t, 1)) for n in (W_OP, B_OP, K_OP, R_OP))
            a8 = _row(ops_s, A_OP, j, pl.ds(t_next, 1))
            for q in range(N_STRIP):
                rows = pl.ds(q * SUB, SUB)
                s_new = s_s[j, rows, :] * w8 + sa[q] * b8 + v[q] * k8
                s_s[j, rows, :] = s_new
                yq, aq = s_new * r8, s_new * a8
                y[q] = yq if y[q] is None else y[q] + yq
                sa_next[q] = aq if sa_next[q] is None else sa_next[q] + aq
        for q in range(N_STRIP):
            ysc_s[t, pl.ds(q * SUB, SUB), :] = y[q]
        return tuple(sa_next)

    lax.fori_loop(0, tt, step, tuple(sa))

    def y_rows(il, carry):
        yt = ysc_s[:, il, :].T
        vt_s[:, 0, il, :] = yt[:inst]
        vt_s[:, 1, il, :] = yt[inst:]
        return carry

    lax.fori_loop(0, I_LO, y_rows, 0, unroll=RELAYOUT_UNROLL)
    for bi in range(bsz):
        y_ref[bi] = vt_s[bi * N_HEADS:(bi + 1) * N_HEADS].reshape(C_A, tt).T

    @pl.when(pl.program_id(0) == pl.num_programs(0) - 1)
    def _():
        sfin_ref[...] = s_s[...]


def _wkv_scan_call(a, w, b, k, r, v, s0, tt):
    bsz, t_len, _ = w.shape
    inst = bsz * N_HEADS
    state = _const((HEAD, I_LO, 2 * HEAD))
    return pl.pallas_call(
        functools.partial(_wkv_scan_kernel, tt),
        grid=(t_len // tt,),
        in_specs=[pl.BlockSpec(memory_space=pl.ANY)] * 6 + [state],
        out_specs=[pl.BlockSpec((bsz, tt, C_A), lambda c: (0, c, 0)), _full((HEAD, I_LO, 2 * HEAD))],
        out_shape=[jax.ShapeDtypeStruct((bsz, t_len, C_A), F32),
                   jax.ShapeDtypeStruct((HEAD, I_LO, 2 * HEAD), F32)],
        scratch_shapes=[pltpu.VMEM((HEAD, I_LO, 2 * HEAD), F32),
                        pltpu.VMEM((6, bsz, tt, C_A), F32),
                        pltpu.SemaphoreType.DMA((6,)),
                        pltpu.VMEM((5, HEAD, tt, 2 * inst), F32),
                        pltpu.VMEM((inst, HEAD, tt), F32),
                        pltpu.VMEM((tt, I_LO, 2 * inst), F32),
                        pltpu.VMEM((inst, 2, I_LO, tt), F32),
                        pltpu.VMEM((tt, I_LO, 2 * inst), F32)],
        compiler_params=_params(),
        name="wkv_scan",
    )(a, w, b, k, r, v, s0)


def _wkv_step_kernel(a_ref, w_ref, b_ref, k_ref, r_ref, v_ref, s_ref, y_ref, so_ref):
    s = s_ref[...]
    row = lax.broadcasted_iota(jnp.int32, (HEAD, HEAD), 0)
    col = lax.broadcasted_iota(jnp.int32, (HEAD, HEAD), 1)
    eye = (row == col).astype(F32)
    sa = jnp.sum(s * a_ref[...], axis=-1, keepdims=True)
    v_col = jnp.sum(eye * v_ref[...], axis=-1, keepdims=True)
    s_new = s * w_ref[...] + sa * b_ref[...] + v_col * k_ref[...]
    so_ref[...] = s_new
    y_col = jnp.sum(s_new * r_ref[...], axis=-1, keepdims=True)
    y_ref[...] = jnp.sum(eye * y_col, axis=-2, keepdims=True)


def _wkv_step_call(a, w, b, k, r, v, s, bb):
    bsz = s.shape[0]
    op = pl.BlockSpec((bb, N_HEADS, 1, HEAD), lambda c: (c, 0, 0, 0))
    st = pl.BlockSpec((bb, N_HEADS, HEAD, HEAD), lambda c: (c, 0, 0, 0))
    return pl.pallas_call(
        _wkv_step_kernel,
        grid=(bsz // bb,),
        in_specs=[op] * 6 + [st],
        out_specs=[op, st],
        out_shape=[jax.ShapeDtypeStruct((bsz, N_HEADS, 1, HEAD), F32),
                   jax.ShapeDtypeStruct((bsz, N_HEADS, HEAD, HEAD), F32)],
        compiler_params=_params(),
        name="wkv_step",
    )(a, w, b, k, r, v, s)


def _route(logits):
    lane = lax.broadcasted_iota(jnp.int32, logits.shape, 1)
    neg = jnp.float32(-jnp.inf)
    big = jnp.int32(ROUTER_LANES)
    is_grp = (lane >= N_EXPERTS) & (lane < N_EXPERTS + N_GROUPS)
    gl = jnp.where(is_grp, logits, neg)
    gmax = jnp.max(gl, axis=-1, keepdims=True)
    ge = jnp.where(is_grp, jnp.exp(gl - gmax), 0.0)
    gp = ge / jnp.sum(ge, axis=-1, keepdims=True)
    gi = jnp.min(jnp.where(gl == gmax, lane, big), axis=-1, keepdims=True)
    p_grp = jnp.sum(jnp.where(lane == gi, gp, 0.0), axis=-1, keepdims=True)
    gidx = gi - N_EXPERTS
    in_grp = (lane >= gidx * EXP_PER_GROUP) & (lane < (gidx + 1) * EXP_PER_GROUP)
    el = jnp.where(in_grp, logits, neg)
    emax = jnp.max(el, axis=-1, keepdims=True)
    ee = jnp.where(in_grp, jnp.exp(el - emax), 0.0)
    pe = ee / jnp.sum(ee, axis=-1, keepdims=True)
    pe_m = jnp.where(in_grp, pe, -1.0)
    v1 = jnp.max(pe_m, axis=-1, keepdims=True)
    i1 = jnp.min(jnp.where(pe_m == v1, lane, big), axis=-1, keepdims=True)
    pe_m2 = jnp.where(lane == i1, -1.0, pe_m)
    v2 = jnp.max(pe_m2, axis=-1, keepdims=True)
    i2 = jnp.min(jnp.where(pe_m2 == v2, lane, big), axis=-1, keepdims=True)
    tot = v1 + v2
    return jnp.where(lane == i1, v1 / tot * p_grp, 0.0) + jnp.where(lane == i2, v2 / tot * p_grp, 0.0)


def _mix_post_kernel(precise, batch_major, tt, bsz,
                     ya_ref, bonus_ref, g_ref, yb_ref, x_ref, g1_ref, sh2_ref, sc2_ref, wout_ref, *refs):
    wout_lo = refs[0] if precise else None
    (lnxg_ref, lnxb_ref, gones_ref, ln1g_ref, ln1b_ref, wrh_ref, wrl_ref, br_ref,
     x1_o, h2_o, comb_o, ya_tm_s, *tm_s) = refs[1:] if precise else refs

    def wout(lo_hi_ref, r0, r1):
        return None if lo_hi_ref is None else lo_hi_ref[r0:r1, :]

    def mix(t0, nt):
        nrows = nt * bsz
        steps = pl.ds(t0, nt)

        def gsum(z):
            return _group_sum(z.reshape(nrows, C_A), gones_ref[...], precise).reshape(nt, bsz, C_A)

        ya = _load_batch_major(ya_ref, ya_tm_s, t0, nt)
        mu = gsum(ya) * (1.0 / HEAD)
        yc = ya - mu
        var = gsum(yc * yc) * (1.0 / HEAD)
        yn = yc * lax.rsqrt(var + GN_EPS) * lnxg_ref[...] + lnxb_ref[...]
        ya = (yn + bonus_ref[steps]) * g_ref[steps]
        return (_mm(_split(ya.reshape(nrows, C_A), precise), wout_ref[0:C_A, :], wout(wout_lo, 0, C_A))
                + _mm(_split(yb_ref[steps].reshape(nrows, C_B), precise), wout_ref[C_A:, :],
                      wout(wout_lo, C_A, C_A + C_B))).reshape(nt, bsz, D_MODEL)

    def post(t0, nt, y):
        nrows = nt * bsz
        steps = pl.ds(t0, nt)
        out_rows = pl.ds(t0 * bsz, nrows)
        x = _load_batch_major(x_ref, tm_s[0], t0, nt) if batch_major else x_ref[steps]
        x1 = _layer_norm(ALPHA * x + (1.0 + g1_ref[...]) * y) * ln1g_ref[...] + ln1b_ref[...]
        x1_o[steps] = x1
        h2 = (_layer_norm(x1) * (1.0 + sc2_ref[...]) + sh2_ref[...]).reshape(nrows, D_MODEL)
        hi = h2.astype(BF16)
        lo = (h2 - hi.astype(F32)).astype(BF16)
        h2_o[out_rows, :] = h2 if h2_o.dtype == F32 else hi
        logits = _dot(hi, wrh_ref[...]) + _dot(lo, wrh_ref[...]) + _dot(hi, wrl_ref[...]) + br_ref[...]
        comb_o[out_rows, :] = _route(logits)

    n_parts = MIX_PARTS if tt % MIX_PARTS == 0 else 1
    nt = tt // n_parts
    y = mix(0, nt)
    for i in range(n_parts):
        y_next = mix((i + 1) * nt, nt) if i + 1 < n_parts else None
        post(i * nt, nt, y)
        y = y_next


def _mix_post_call(ya, bonus, g, yb, x, batch_major, tt, g1, sh2, sc2, wl):
    bsz, t_len, _ = ya.shape
    rows = tt * bsz
    ya_spec = pl.BlockSpec((bsz, tt, C_A), lambda c: (0, c, 0))
    if batch_major:
        x_spec = pl.BlockSpec((bsz, tt, D_MODEL), lambda c: (0, c, 0))
    else:
        x_spec = pl.BlockSpec((tt, bsz, D_MODEL), lambda c: (c, 0, 0))
    seq = pl.BlockSpec((tt, bsz, C_A), lambda c: (c, 0, 0))
    precise = wl['precise']
    weights = [wl['w_out']] + ([wl['w_out_lo']] if precise else []) + [
        wl['lnx_gain'], wl['lnx_bias'], wl['gones'], wl['ln1_gain'], wl['ln1_bias'],
        wl['w_router_hi'], wl['w_router_lo'], wl['b_router']]
    scratch = [pltpu.VMEM((tt, bsz, C_A), F32)] + ([pltpu.VMEM((tt, bsz, D_MODEL), F32)] if batch_major else [])
    return pl.pallas_call(
        functools.partial(_mix_post_kernel, precise, batch_major, tt, bsz),
        grid=(t_len // tt,),
        in_specs=[ya_spec] + [seq] * 3 + [x_spec] + [_full((bsz, D_MODEL))] * 3 + [_const(w.shape) for w in weights],
        out_specs=[pl.BlockSpec((tt, bsz, D_MODEL), lambda c: (c, 0, 0)),
                   pl.BlockSpec((rows, D_MODEL), lambda c: (c, 0)),
                   pl.BlockSpec((rows, ROUTER_LANES), lambda c: (c, 0))],
        out_shape=[jax.ShapeDtypeStruct((t_len, bsz, D_MODEL), F32),
                   jax.ShapeDtypeStruct((t_len * bsz, D_MODEL), F32 if wl['moe_precise'] else BF16),
                   jax.ShapeDtypeStruct((t_len * bsz, ROUTER_LANES), F32)],
        scratch_shapes=scratch,
        compiler_params=_params(),
        name="mix_post",
    )(ya, bonus, g, yb, x, g1, sh2, sc2, *weights)


def _moe_kernel(precise, batch_major_out, tt, bsz, h2_ref, comb_ref, x1_ref, g2_ref, *refs):
    wg_ref, wu_ref, wd_ref, ln2g_ref, ln2b_ref, o_ref, acc_s = refs
    weight = lambda r, i: _split(r[i], precise)
    step = pl.program_id(1)
    rows = tt * bsz

    @pl.when(step == 0)
    def _():
        acc_s[...] = jnp.zeros_like(acc_s)

    h = _split(h2_ref[...], True) if precise else (h2_ref[...], None)
    lane = lax.broadcasted_iota(jnp.int32, (rows, ROUTER_LANES), 1)
    comb = comb_ref[...]
    down = None
    for i in range(MOE_EXPERTS_PER_STEP):
        ce = jnp.sum(jnp.where(lane == step * MOE_EXPERTS_PER_STEP + i, comb, 0.0), axis=-1, keepdims=True)
        gate = _mm(h, *weight(wg_ref, i))
        hid = (gate * _sigmoid(gate)) * _mm(h, *weight(wu_ref, i))
        part = _mm(_split(hid * ce, precise), *weight(wd_ref, i))
        down = part if down is None else down + part
    acc_s[...] += down

    @pl.when(step == N_EXPERTS // MOE_EXPERTS_PER_STEP - 1)
    def _():
        moe = acc_s[...].reshape(tt, bsz, D_MODEL)
        out = _layer_norm(ALPHA * x1_ref[...] + (1.0 + g2_ref[...]) * moe) * ln2g_ref[...] + ln2b_ref[...]
        if batch_major_out:
            _store_batch_major(o_ref, out, 0, tt)
        else:
            o_ref[...] = out


def _moe_call(h2, comb, x1, batch_major_out, tt, g2, wl):
    precise = wl['moe_precise']
    t_len, bsz, _ = x1.shape
    rows = tt * bsz
    first = wl['layer'] * (N_EXPERTS // MOE_EXPERTS_PER_STEP)
    w_gate = pl.BlockSpec((MOE_EXPERTS_PER_STEP, D_MODEL, D_EXPERT), lambda c, e: (first + e, 0, 0))
    w_down = pl.BlockSpec((MOE_EXPERTS_PER_STEP, D_EXPERT, D_MODEL), lambda c, e: (first + e, 0, 0))
    weights = [wl['w_exp_gate'], wl['w_exp_up'], wl['w_exp_down']]
    w_specs = [w_gate, w_gate, w_down]
    if batch_major_out:
        o_spec = pl.BlockSpec((bsz, tt, D_MODEL), lambda c, e: (0, c, 0))
        o_shape = jax.ShapeDtypeStruct((bsz, t_len, D_MODEL), F32)
    else:
        o_spec = pl.BlockSpec((tt, bsz, D_MODEL), lambda c, e: (c, 0, 0))
        o_shape = jax.ShapeDtypeStruct((t_len, bsz, D_MODEL), F32)
    return pl.pallas_call(
        functools.partial(_moe_kernel, precise, batch_major_out, tt, bsz),
        grid=(t_len // tt, N_EXPERTS // MOE_EXPERTS_PER_STEP),
        in_specs=[pl.BlockSpec((rows, D_MODEL), lambda c, e: (c, 0)),
                  pl.BlockSpec((rows, ROUTER_LANES), lambda c, e: (c, 0)),
                  pl.BlockSpec((tt, bsz, D_MODEL), lambda c, e: (c, 0, 0)),
                  pl.BlockSpec((bsz, D_MODEL), lambda c, e: (0, 0))] + w_specs + [
                  pl.BlockSpec((1, D_MODEL), lambda c, e: (0, 0)),
                  pl.BlockSpec((1, D_MODEL), lambda c, e: (0, 0))],
        out_specs=o_spec,
        out_shape=o_shape,
        scratch_shapes=[pltpu.VMEM((rows, D_MODEL), F32)],
        compiler_params=_params(2, MOE_VMEM_LIMIT),
        name="moe",
    )(h2, comb, x1, g2, *weights, wl['ln2_gain'], wl['ln2_bias'])


def _pad_cols(w, width):
    return jnp.pad(w, ((0, 0), (0, width - w.shape[1])))


def _pad_rows(w, height):
    return jnp.pad(w, ((0, height - w.shape[0]), (0, 0)))


def _split_weight_kernel(w_ref, hi_ref, lo_ref):
    hi_ref[...], lo_ref[...] = _split(w_ref[...])


def _split_weight(w):
    w2 = w.reshape(-1, w.shape[-1])
    rows, cols = w2.shape
    tr = min(rows, 512)
    spec = pl.BlockSpec((tr, cols), lambda i: (i, 0))
    hi, lo = pl.pallas_call(
        _split_weight_kernel,
        grid=(rows // tr,),
        in_specs=[spec],
        out_specs=[spec, spec],
        out_shape=[jax.ShapeDtypeStruct(w2.shape, BF16)] * 2,
        compiler_params=_params(),
        name="split_weight",
    )(w2)
    return hi.reshape(w.shape), lo.reshape(w.shape)


def _block_diag(w):
    n, c, d = w.shape
    eye = jnp.eye(n, dtype=w.dtype)
    return (eye[:, None, :, None] * w[:, :, None, :]).reshape(n * c, n * d)


def _prep_layer(p, l, precise, moe_precise):
    row = lambda v: v[l].reshape(1, -1)
    w_in = p['w_in'][l]
    o1, o2, o3 = 3 * C_A, 3 * C_A + LORA_DECAY, 3 * C_A + LORA_DECAY + LORA_AAA
    pieces = [(0, o1, o1), (o1, o2, LORA_PAD), (o2, o3, LORA_PAD), (o3, N_SHIFT, LORA_PAD)]
    w_in_p = jnp.concatenate([_pad_cols(w_in[:, a:b], wd) for a, b, wd in pieces] + [w_in[:, N_SHIFT:]], axis=1)
    mu = p['mu_shift'][l].reshape(1, -1)
    mu_p = jnp.concatenate([_pad_cols(mu[:, a:b], wd) for a, b, wd in pieces], axis=1)
    w_router = jnp.concatenate([p['w_router_expert'][l], p['w_router_group'][l]], axis=1)
    w_router = _pad_cols(w_router, ROUTER_LANES)
    w_router_hi, w_router_lo = _split_weight(w_router)
    b_router = _pad_cols(jnp.concatenate([p['b_router_expert'][l], p['b_router_group'][l]]).reshape(1, -1),
                         ROUTER_LANES)
    head_id = jnp.arange(C_A) // HEAD
    mats = dict(
        w_in=w_in_p,
        w_decay_up=_pad_rows(p['w_decay_up'][l], LORA_PAD),
        w_a_up=_pad_rows(p['w_a_up'][l], LORA_PAD),
        w_g_up=_pad_rows(p['w_g_up'][l], LORA_PAD),
        w_gates=jnp.concatenate([_block_diag(p['w_rgate'][l]), _block_diag(p['w_igate'][l])], axis=1),
        w_out=p['w_out'][l])
    split_mats = {}
    for name, w in mats.items():
        if precise:
            split_mats[name], split_mats[name + '_lo'] = _split_weight(w)
        else:
            split_mats[name] = w.astype(BF16)
    for name in ('w_exp_gate', 'w_exp_up', 'w_exp_down'):
        split_mats[name] = p[name].reshape((-1,) + p[name].shape[2:])
    return dict(
        split_mats, layer=l, precise=precise, moe_precise=moe_precise, mu=mu_p, w0=row(p['w0']), a0=row(p['a0']),
        k_k=row(p['k_k']), k_a=row(p['k_a']), r_k=row(p['r_k']),
        conv_w=p['conv_w'][l], conv_b=row(p['conv_b']),
        b_gates=jnp.concatenate([p['b_rgate'][l], p['b_igate'][l]]).reshape(1, -1),
        lam=row(p['lru_lambda']),
        gones=(head_id[:, None] == head_id[None, :]).astype(BF16),
        lnx_gain=row(p['lnx_gain']), lnx_bias=row(p['lnx_bias']),
        ln1_gain=row(p['ln1_gain']), ln1_bias=row(p['ln1_bias']),
        w_router_hi=w_router_hi, w_router_lo=w_router_lo,
        b_router=b_router,
        ln2_gain=row(p['ln2_gain']), ln2_bias=row(p['ln2_bias']),
    )


def _trunk(x, batch_major, tt, mods, st_wkv, st_shift, st_conv, st_lru, layers, moe_tt, scan_tt):
    wkv_out, shift_out, conv_out, lru_out = [], [], [], []
    for l in range(DEPTH):
        wl = layers[l]
        sh1, sc1, g1, sh2, sc2, g2 = mods[l]
        bm_in = batch_major and l == 0
        conv_state = jnp.swapaxes(st_conv[l], 0, 1)
        (r, w, k, v, a, b, g, bonus, yb, shift_new, conv_new, lru_new) = _mix_pre_call(
            x, bm_in, tt, sh1, sc1, st_shift[l], conv_state, st_lru[l], wl)
        bsz, t_len, _ = r.shape
        if t_len > 1:
            s0 = st_wkv[l].reshape(bsz, N_HEADS, 2, I_LO, HEAD).transpose(4, 3, 2, 0, 1)
            s0 = s0.reshape(HEAD, I_LO, 2 * HEAD)
            ya, s_fin = _wkv_scan_call(a, w, b, k, r, v, s0, scan_tt)
            s_new = s_fin.reshape(HEAD, I_LO, 2, bsz, N_HEADS).transpose(3, 4, 2, 1, 0)
            s_new = s_new.reshape(bsz, N_HEADS, HEAD, HEAD)
        else:
            shp = (bsz, N_HEADS, 1, HEAD)
            y4, s_new = _wkv_step_call(a.reshape(shp), w.reshape(shp), b.reshape(shp), k.reshape(shp),
                                       r.reshape(shp), v.reshape(shp), st_wkv[l], WKV_STEP_BATCH)
            ya = y4.reshape(bsz, 1, C_A)
        post_tt = POST_TT if t_len % POST_TT == 0 else tt
        x1, h2, comb = _mix_post_call(ya, bonus, g, yb, x, bm_in, post_tt, g1, sh2, sc2, wl)
        bm_out = batch_major and l == DEPTH - 1
        x = _moe_call(h2, comb, x1, bm_out, moe_tt, g2, wl)
        wkv_out.append(s_new)
        shift_out.append(shift_new)
        conv_out.append(jnp.swapaxes(conv_new, 0, 1))
        lru_out.append(lru_new)
    return x, jnp.stack(wkv_out), jnp.stack(shift_out), jnp.stack(conv_out), jnp.stack(lru_out)


def kernel(x_prompt, x_sample, c_prompt, c_sample, state_wkv, state_shift, state_conv, state_lru, w_ada, b_ada, w_in, mu_shift, w0, w_decay_up, a0, w_a_up, w_g_up, k_k, k_a, r_k, lnx_gain, lnx_bias, conv_w, conv_b, w_rgate, b_rgate, w_igate, b_igate, lru_lambda, w_out, ln1_gain, ln1_bias, w_router_group, b_router_group, w_router_expert, b_router_expert, w_exp_gate, w_exp_up, w_exp_down, ln2_gain, ln2_bias):
    p = dict(w_in=w_in, mu_shift=mu_shift, w0=w0, w_decay_up=w_decay_up, a0=a0, w_a_up=w_a_up,
             w_g_up=w_g_up, k_k=k_k, k_a=k_a, r_k=r_k.reshape(DEPTH, C_A), lnx_gain=lnx_gain,
             lnx_bias=lnx_bias, conv_w=conv_w, conv_b=conv_b, w_rgate=w_rgate, b_rgate=b_rgate,
             w_igate=w_igate, b_igate=b_igate, lru_lambda=lru_lambda, w_out=w_out, ln1_gain=ln1_gain,
             ln1_bias=ln1_bias, w_router_group=w_router_group, b_router_group=b_router_group,
             w_router_expert=w_router_expert, b_router_expert=b_router_expert, w_exp_gate=w_exp_gate,
             w_exp_up=w_exp_up, w_exp_down=w_exp_down, ln2_gain=ln2_gain, ln2_bias=ln2_bias)
    prepared = [_prep_layer(p, l, precise=True, moe_precise=(l < DEPTH - 1)) for l in range(DEPTH)]
    layers_s = prepared
    layers_p = [dict(prepared[l], precise=(l == 0), moe_precise=False) for l in range(DEPTH)]
    bp, bs = x_prompt.shape[0], x_sample.shape[0]

    mod = _ada_call(jnp.concatenate([c_sample, c_prompt], axis=0), w_ada, b_ada)
    split = lambda m: [m[:, i * D_MODEL:(i + 1) * D_MODEL] for i in range(6)]
    mods_s = [split(mod[l, :bs]) for l in range(DEPTH)]
    mods_p = [split(mod[l, bs:bs + bp]) for l in range(DEPTH)]

    z_wkv = jnp.zeros((DEPTH, bp, N_HEADS, HEAD, HEAD), F32)
    z_shift = jnp.zeros((DEPTH, bp, D_MODEL), F32)
    z_conv = jnp.zeros((DEPTH, bp, CONV_W - 1, C_B), F32)
    z_lru = jnp.zeros((DEPTH, bp, C_B), F32)
    y_p, wkv_p, shift_p, conv_p, lru_p = _trunk(x_prompt, True, MIX_TT, mods_p, z_wkv, z_shift, z_conv, z_lru,
                                                layers_p, MOE_TT, SCAN_TT)
    xs_tm = x_sample.reshape(1, bs, D_MODEL)
    y_s, wkv_s, shift_s, conv_s, lru_s = _trunk(xs_tm, False, 1, mods_s, state_wkv, state_shift, state_conv,
                                                state_lru, layers_s, 1, 1)
    return (y_p, y_s.reshape(bs, 1, D_MODEL), wkv_p, shift_p, conv_p, lru_p, wkv_s, shift_s, conv_s, lru_s)
```

```python
import functools

import jax
import jax.numpy as jnp
from jax import lax
from jax.experimental import pallas as pl
from jax.experimental.pallas import tpu as pltpu

F32 = jnp.float32
BF16 = jnp.bfloat16

D_MODEL = 1024
DEPTH = 2
C_A = 512
C_B = 512
HEAD = 64
N_HEADS = C_A // HEAD
LORA_DECAY = 32
LORA_AAA = 32
LORA_GATE = 96
LORA_PAD = 128
N_SHIFT = 3 * C_A + LORA_DECAY + LORA_AAA + LORA_GATE
N_SHIFT_PAD = 3 * C_A + 3 * LORA_PAD
N_IN_PAD = N_SHIFT_PAD + 2 * C_B
CONV_W = 4
LRU_BLOCKS = 8
LRU_C = 8.0
N_GROUPS = 4
EXP_PER_GROUP = 4
N_EXPERTS = 16
MOE_EXPERTS_PER_STEP = 4
D_EXPERT = 256
ROUTER_LANES = 128
ALPHA = (2 * DEPTH) ** 0.25
LN_EPS = 1e-5
GN_EPS = 64e-5
VMEM_LIMIT = 56 * 1024 * 1024
MOE_VMEM_LIMIT = 62 * 1024 * 1024
MIX_TT = 64
POST_TT = 128
MOE_TT = 128
SCAN_TT = 128
WKV_STEP_BATCH = 16


def _params(n_axes=1, vmem_limit=VMEM_LIMIT):
    return pltpu.CompilerParams(dimension_semantics=("arbitrary",) * n_axes,
                                vmem_limit_bytes=vmem_limit)


def _full(shape):
    return pl.BlockSpec(shape, lambda *_: (0,) * len(shape))


def _const(shape):
    return pl.BlockSpec(shape, lambda *_: (0,) * len(shape), pipeline_mode=pl.Buffered(1))


def _layer_norm(x):
    mu = jnp.mean(x, axis=-1, keepdims=True)
    xc = x - mu
    var = jnp.mean(xc * xc, axis=-1, keepdims=True)
    return xc * lax.rsqrt(var + LN_EPS)


def _softplus(z):
    return jnp.maximum(z, 0.0) + jnp.log1p(jnp.exp(-jnp.abs(z)))


def _sigmoid(z):
    return 1.0 / (1.0 + jnp.exp(-z))


def _gelu_tanh(x):
    c = 0.7978845608028654
    return x * (0.5 * (1.0 + jnp.tanh(c * (x + 0.044715 * (x * x * x)))))


def _dot(a, b):
    return jnp.dot(a, b, preferred_element_type=F32)


def _split(x, precise=True):
    hi = x.astype(BF16)
    return hi, ((x - hi.astype(F32)).astype(BF16) if precise else None)


def _mm(xs, w_hi, w_lo=None):
    hi, lo = xs
    out = _dot(hi, w_hi)
    if w_lo is not None:
        out = out + _dot(lo, w_hi) + _dot(hi, w_lo)
    return out


def _group_sum(x, ones_bf16, precise):
    hi, lo = _split(x, precise)
    out = _dot(hi, ones_bf16)
    return out + _dot(lo, ones_bf16) if precise else out


def _load_batch_major(x_ref, tm_ref, t0, nt):
    for t in range(t0, t0 + nt):
        tm_ref[t] = x_ref[:, t, :]
    return tm_ref[pl.ds(t0, nt)]


def _store_batch_major(o_ref, val, t0, nt):
    for t in range(nt):
        o_ref[:, t0 + t, :] = val[t]


def _ada_kernel(c_ref, w_ref, b_ref, o_ref):
    c = c_ref[...]
    w_hi, w_lo = _split(w_ref[0])
    o_ref[0] = _mm(_split(c * _sigmoid(c)), w_hi, w_lo) + b_ref[0]


def _ada_call(c_all, w_ada, b_ada):
    n = c_all.shape[0]
    tn = 1536
    return pl.pallas_call(
        _ada_kernel,
        grid=(DEPTH, 6 * D_MODEL // tn),
        in_specs=[pl.BlockSpec((n, D_MODEL), lambda l, j: (0, 0)),
                  pl.BlockSpec((1, D_MODEL, tn), lambda l, j: (l, 0, j)),
                  pl.BlockSpec((1, 1, tn), lambda l, j: (l, 0, j))],
        out_specs=pl.BlockSpec((1, n, tn), lambda l, j: (l, 0, j)),
        out_shape=jax.ShapeDtypeStruct((DEPTH, n, 6 * D_MODEL), F32),
        compiler_params=_params(2),
        name="ada_mod",
    )(c_all, w_ada, b_ada.reshape(DEPTH, 1, 6 * D_MODEL))


MIX_PARTS = 2
N_PRE_MATS = 5


def _mix_pre_kernel(precise, batch_major, tt, bsz,
                    x_ref, sh_ref, sc_ref, hprev_ref, cst_ref, lst_ref, *refs):
    n_mat = N_PRE_MATS * (2 if precise else 1)
    mats = refs[:n_mat]
    if precise:
        (win_ref, win_lo), (wdu_ref, wdu_lo), (wau_ref, wau_lo), (wgu_ref, wgu_lo), (wg_ref, wg_lo) = (
            (mats[2 * i], mats[2 * i + 1]) for i in range(N_PRE_MATS))
    else:
        (win_ref, win_lo), (wdu_ref, wdu_lo), (wau_ref, wau_lo), (wgu_ref, wgu_lo), (wg_ref, wg_lo) = (
            (m, None) for m in mats)
    (mu_ref, w0_ref, a0_ref, kk_ref, ka_ref, rk_ref, cw_ref, cb_ref, bg_ref, lam_ref, gones_ref,
     r_o, w_o, k_o, v_o, a_o, b_o, g_o, bonus_o, yb_o, shift_o, conv_o, lru_o,
     prevp_s, conv_s, lru_s, a_s, u_s, hs_s, *tm_s) = refs[n_mat:]

    def win(lo_hi_ref, c0, c1):
        return None if lo_hi_ref is None else lo_hi_ref[:, c0:c1]

    @pl.when(pl.program_id(0) == 0)
    def _():
        prevp_s[...] = _mm(_split(hprev_ref[...], precise), win_ref[:, :N_SHIFT_PAD], win(win_lo, 0, N_SHIFT_PAD))
        conv_s[...] = cst_ref[...]
        lru_s[...] = lst_ref[...]

    col_blocks = ((0, C_A), (C_A, C_A), (2 * C_A, C_A), (3 * C_A, 3 * LORA_PAD),
                  (N_SHIFT_PAD, C_B), (N_SHIFT_PAD + C_B, C_B))

    def project(t0, nt):
        nrows = nt * bsz
        x = _load_batch_major(x_ref, tm_s[0], t0, nt) if batch_major else x_ref[pl.ds(t0, nt)]
        h = _layer_norm(x) * (1.0 + sc_ref[...]) + sh_ref[...]
        if t0 + nt == tt:
            shift_o[...] = h[nt - 1]
        hb = _split(h.reshape(nrows, D_MODEL), precise)
        return [_mm(hb, win_ref[:, off:off + width], win(win_lo, off, off + width)).reshape(nt, bsz, width)
                for off, width in col_blocks]

    def finish(t0, nt, proj):
        nrows = nt * bsz
        steps = pl.ds(t0, nt)

        def shifted(idx):
            off, width = col_blocks[idx]
            p = proj[idx]
            first = prevp_s[:, off:off + width][None]
            prev = jnp.concatenate([first, p[:nt - 1]], axis=0) if nt > 1 else first
            prevp_s[:, off:off + width] = p[nt - 1]
            return p + (prev - p) * mu_ref[:, off:off + width]

        def up(z, w_ref_, w_lo_):
            zs = _split(z.reshape(nrows, LORA_PAD), precise)
            return _mm(zs, w_ref_[...], None if w_lo_ is None else w_lo_[...]).reshape(nt, bsz, C_A)

        def gsum(z):
            return _group_sum(z.reshape(nrows, C_A), gones_ref[...], precise).reshape(nt, bsz, C_A)

        r, k, v, lora = (shifted(idx) for idx in range(4))
        gate_b, rec_b = proj[4], proj[5]
        wd = lora[:, :, 0:LORA_PAD]
        ad = lora[:, :, LORA_PAD:2 * LORA_PAD]
        gd = lora[:, :, 2 * LORA_PAD:3 * LORA_PAD]

        w_log = -_softplus(-(w0_ref[...] + up(jnp.tanh(wd), wdu_ref, wdu_lo))) - 0.5
        decay = jnp.exp(-jnp.exp(w_log))
        a_sig = _sigmoid(a0_ref[...] + up(ad, wau_ref, wau_lo))
        g_o[steps] = up(_sigmoid(gd), wgu_ref, wgu_lo)

        kk = k * kk_ref[...]
        kk = kk / jnp.maximum(jnp.sqrt(gsum(kk * kk)), 1e-12)
        k2 = k * (1.0 + (a_sig - 1.0) * ka_ref[...])
        for o_ref, val in ((r_o, r), (w_o, decay), (k_o, k2), (v_o, v), (a_o, -kk), (b_o, kk * a_sig)):
            o_ref[steps] = val
        bonus_o[steps] = gsum(r * k2 * rk_ref[...]) * v

        xp = jnp.concatenate([conv_s[...], rec_b], axis=0)
        xc = cb_ref[...] + xp[0:nt] * cw_ref[0:1, :]
        for j in range(1, CONV_W):
            xc = xc + xp[j:j + nt] * cw_ref[j:j + 1, :]
        conv_s[...] = xp[nt:nt + CONV_W - 1]

        gates = _mm(_split(xc.reshape(nrows, C_B), precise), wg_ref[...],
                    None if wg_lo is None else wg_lo[...]) + bg_ref[...]
        r_t = _sigmoid(gates[:, :C_B]).reshape(nt, bsz, C_B)
        i_t = _sigmoid(gates[:, C_B:]).reshape(nt, bsz, C_B)
        log_a = -LRU_C * r_t * _softplus(-lam_ref[...])
        a_s[steps] = jnp.exp(log_a)
        u_s[steps] = jnp.sqrt(1.0 - jnp.exp(2.0 * log_a)) * (i_t * xc)
        yb_o[steps] = _gelu_tanh(gate_b)

    n_parts = MIX_PARTS if tt % MIX_PARTS == 0 else 1
    nt = tt // n_parts
    proj = project(0, nt)
    for i in range(n_parts):
        proj_next = project((i + 1) * nt, nt) if i + 1 < n_parts else None
        finish(i * nt, nt, proj)
        proj = proj_next
    conv_o[...] = conv_s[...]

    def scan_step(t, hcur):
        hcur = a_s[t] * hcur + u_s[t]
        hs_s[t] = hcur
        return hcur

    h_fin = lax.fori_loop(0, tt, scan_step, lru_s[...])
    lru_s[...] = h_fin
    lru_o[...] = h_fin
    yb_o[...] = hs_s[...] * yb_o[...]


def _mix_pre_call(x, batch_major, tt, sh1, sc1, h_prev, conv_state, lru_state, wl):
    precise = wl['precise']
    if batch_major:
        bsz, t_len, _ = x.shape
        x_spec = pl.BlockSpec((bsz, tt, D_MODEL), lambda c: (0, c, 0))
    else:
        t_len, bsz, _ = x.shape
        x_spec = pl.BlockSpec((tt, bsz, D_MODEL), lambda c: (c, 0, 0))
    seq = lambda ch: pl.BlockSpec((tt, bsz, ch), lambda c: (c, 0, 0))
    weights = []
    for name in ('w_in', 'w_decay_up', 'w_a_up', 'w_g_up', 'w_gates'):
        weights.append(wl[name])
        if precise:
            weights.append(wl[name + '_lo'])
    weights += [wl['mu'], wl['w0'], wl['a0'], wl['k_k'], wl['k_a'], wl['r_k'], wl['conv_w'], wl['conv_b'],
                wl['b_gates'], wl['lam'], wl['gones']]
    seq_out = jax.ShapeDtypeStruct((t_len, bsz, C_A), F32)
    scan_spec, scan_out = seq(C_A), seq_out
    scratch = [pltpu.VMEM((bsz, N_SHIFT_PAD), F32), pltpu.VMEM((CONV_W - 1, bsz, C_B), F32),
               pltpu.VMEM((bsz, C_B), F32), pltpu.VMEM((tt, bsz, C_B), F32),
               pltpu.VMEM((tt, bsz, C_B), F32), pltpu.VMEM((tt, bsz, C_B), F32)]
    if batch_major:
        scratch.append(pltpu.VMEM((tt, bsz, D_MODEL), F32))
    return pl.pallas_call(
        functools.partial(_mix_pre_kernel, precise, batch_major, tt, bsz),
        grid=(t_len // tt,),
        in_specs=[x_spec, _full((bsz, D_MODEL)), _full((bsz, D_MODEL)), _full((bsz, D_MODEL)),
                  _full((CONV_W - 1, bsz, C_B)), _full((bsz, C_B))] + [_const(w.shape) for w in weights],
        out_specs=[scan_spec] * 6 + [seq(C_A)] * 3 + [_full((bsz, D_MODEL)), _full((CONV_W - 1, bsz, C_B)),
                                                      _full((bsz, C_B))],
        out_shape=[scan_out] * 6 + [seq_out] * 3 + [jax.ShapeDtypeStruct((bsz, D_MODEL), F32),
                                   jax.ShapeDtypeStruct((CONV_W - 1, bsz, C_B), F32),
                                   jax.ShapeDtypeStruct((bsz, C_B), F32)],
        scratch_shapes=scratch,
        compiler_params=_params(),
        name="mix_pre",
    )(x, sh1, sc1, h_prev, conv_state, lru_state, *weights)


I_LO = HEAD // 2
SUB = 8
N_STRIP = I_LO // SUB


def _strips(ref, *lead):
    return [ref[(*lead, pl.ds(q * SUB, SUB), slice(None))] for q in range(N_STRIP)]


def _row(ref, *idx):
    return jnp.broadcast_to(ref[(*idx, slice(None))], (SUB, 2 * HEAD))


A_OP, W_OP, B_OP, K_OP, R_OP = range(5)
RELAYOUT_UNROLL = 16


V_RAW = 5


def _wkv_scan_kernel(tt, a_hbm, w_hbm, b_hbm, k_hbm, r_hbm, v_hbm, s0_ref,
                     y_ref, sfin_ref, s_s, raw_s, raw_sem, ops_s, zt_s, vop_s, vt_s, ysc_s):
    hbm = (a_hbm, w_hbm, b_hbm, k_hbm, r_hbm, v_hbm)
    bsz = raw_s.shape[2]
    inst = bsz * N_HEADS
    chunk = pl.program_id(0)

    def raw_copy(n, c):
        t0 = pl.multiple_of(c * tt, tt)
        return pltpu.make_async_copy(hbm[n].at[pl.ds(t0, tt)], raw_s.at[n], raw_sem.at[n])

    @pl.when(chunk == 0)
    def _():
        for n in range(len(hbm)):
            raw_copy(n, 0).start()
        s_s[...] = s0_ref[...]

    for n in range(V_RAW):
        raw_copy(n, chunk).wait()
        for bi in range(bsz):
            zt_s[bi * N_HEADS:(bi + 1) * N_HEADS] = raw_s[n, :, bi, :].T.reshape(N_HEADS, HEAD, tt)

        def key_rows(j, carry, n=n):
            rows = zt_s[:, j, :]
            ops_s[n, j] = jnp.concatenate([rows, rows], axis=0).T
            return carry

        lax.fori_loop(0, HEAD, key_rows, 0, unroll=RELAYOUT_UNROLL)

    raw_copy(V_RAW, chunk).wait()
    for bi in range(bsz):
        vt_s[bi * N_HEADS:(bi + 1) * N_HEADS] = raw_s[V_RAW, :, bi, :].T.reshape(N_HEADS, 2, I_LO, tt)

    def value_rows(il, carry):
        vop_s[:, il, :] = jnp.concatenate([vt_s[:, 0, il, :], vt_s[:, 1, il, :]], axis=0).T
        return carry

    lax.fori_loop(0, I_LO, value_rows, 0, unroll=RELAYOUT_UNROLL)

    @pl.when(chunk + 1 < pl.num_programs(0))
    def _():
        for n in range(len(hbm)):
            raw_copy(n, chunk + 1).start()

    sa = [None] * N_STRIP
    for j in range(HEAD):
        a8 = _row(ops_s, A_OP, j, pl.ds(0, 1))
        for q, s in enumerate(_strips(s_s, j)):
            sa[q] = s * a8 if sa[q] is None else sa[q] + s * a8

    def step(t, sa):
        t_next = jnp.minimum(t + 1, tt - 1)
        v = _strips(vop_s, t)
        y = [None] * N_STRIP
        sa_next = [None] * N_STRIP
        for j in range(HEAD):
            w8, b8, k8, r8 = (_row(ops_s, n, j, pl.ds(t, 1)) for n in (W_OP, B_OP, K_OP, R_OP))
            a8 = _row(ops_s, A_OP, j, pl.ds(t_next, 1))
            for q in range(N_STRIP):
                rows = pl.ds(q * SUB, SUB)
                s_new = s_s[j, rows, :] * w8 + sa[q] * b8 + v[q] * k8
                s_s[j, rows, :] = s_new
                yq, aq = s_new * r8, s_new * a8
                y[q] = yq if y[q] is None else y[q] + yq
                sa_next[q] = aq if sa_next[q] is None else sa_next[q] + aq
        for q in range(N_STRIP):
            ysc_s[t, pl.ds(q * SUB, SUB), :] = y[q]
        return tuple(sa_next)

    lax.fori_loop(0, tt, step, tuple(sa))

    def y_rows(il, carry):
        yt = ysc_s[:, il, :].T
        vt_s[:, 0, il, :] = yt[:inst]
        vt_s[:, 1, il, :] = yt[inst:]
        return carry

    lax.fori_loop(0, I_LO, y_rows, 0, unroll=RELAYOUT_UNROLL)
    for bi in range(bsz):
        y_ref[:, bi, :] = vt_s[bi * N_HEADS:(bi + 1) * N_HEADS].reshape(C_A, tt).T

    @pl.when(pl.program_id(0) == pl.num_programs(0) - 1)
    def _():
        sfin_ref[...] = s_s[...]


def _wkv_scan_call(a, w, b, k, r, v, s0, tt):
    t_len, bsz, _ = w.shape
    inst = bsz * N_HEADS
    state = _const((HEAD, I_LO, 2 * HEAD))
    return pl.pallas_call(
        functools.partial(_wkv_scan_kernel, tt),
        grid=(t_len // tt,),
        in_specs=[pl.BlockSpec(memory_space=pl.ANY)] * 6 + [state],
        out_specs=[pl.BlockSpec((tt, bsz, C_A), lambda c: (c, 0, 0)), _full((HEAD, I_LO, 2 * HEAD))],
        out_shape=[jax.ShapeDtypeStruct((t_len, bsz, C_A), F32),
                   jax.ShapeDtypeStruct((HEAD, I_LO, 2 * HEAD), F32)],
        scratch_shapes=[pltpu.VMEM((HEAD, I_LO, 2 * HEAD), F32),
                        pltpu.VMEM((6, tt, bsz, C_A), F32),
                        pltpu.SemaphoreType.DMA((6,)),
                        pltpu.VMEM((5, HEAD, tt, 2 * inst), F32),
                        pltpu.VMEM((inst, HEAD, tt), F32),
                        pltpu.VMEM((tt, I_LO, 2 * inst), F32),
                        pltpu.VMEM((inst, 2, I_LO, tt), F32),
                        pltpu.VMEM((tt, I_LO, 2 * inst), F32)],
        compiler_params=_params(),
        name="wkv_scan",
    )(a, w, b, k, r, v, s0)


def _wkv_step_kernel(a_ref, w_ref, b_ref, k_ref, r_ref, v_ref, s_ref, y_ref, so_ref):
    s = s_ref[...]
    row = lax.broadcasted_iota(jnp.int32, (HEAD, HEAD), 0)
    col = lax.broadcasted_iota(jnp.int32, (HEAD, HEAD), 1)
    eye = (row == col).astype(F32)
    sa = jnp.sum(s * a_ref[...], axis=-1, keepdims=True)
    v_col = jnp.sum(eye * v_ref[...], axis=-1, keepdims=True)
    s_new = s * w_ref[...] + sa * b_ref[...] + v_col * k_ref[...]
    so_ref[...] = s_new
    y_col = jnp.sum(s_new * r_ref[...], axis=-1, keepdims=True)
    y_ref[...] = jnp.sum(eye * y_col, axis=-2, keepdims=True)


def _wkv_step_call(a, w, b, k, r, v, s, bb):
    bsz = s.shape[0]
    op = pl.BlockSpec((bb, N_HEADS, 1, HEAD), lambda c: (c, 0, 0, 0))
    st = pl.BlockSpec((bb, N_HEADS, HEAD, HEAD), lambda c: (c, 0, 0, 0))
    return pl.pallas_call(
        _wkv_step_kernel,
        grid=(bsz // bb,),
        in_specs=[op] * 6 + [st],
        out_specs=[op, st],
        out_shape=[jax.ShapeDtypeStruct((bsz, N_HEADS, 1, HEAD), F32),
                   jax.ShapeDtypeStruct((bsz, N_HEADS, HEAD, HEAD), F32)],
        compiler_params=_params(),
        name="wkv_step",
    )(a, w, b, k, r, v, s)


def _route(logits):
    lane = lax.broadcasted_iota(jnp.int32, logits.shape, 1)
    neg = jnp.float32(-jnp.inf)
    big = jnp.int32(ROUTER_LANES)
    is_grp = (lane >= N_EXPERTS) & (lane < N_EXPERTS + N_GROUPS)
    gl = jnp.where(is_grp, logits, neg)
    gmax = jnp.max(gl, axis=-1, keepdims=True)
    ge = jnp.where(is_grp, jnp.exp(gl - gmax), 0.0)
    gp = ge / jnp.sum(ge, axis=-1, keepdims=True)
    gi = jnp.min(jnp.where(gl == gmax, lane, big), axis=-1, keepdims=True)
    p_grp = jnp.sum(jnp.where(lane == gi, gp, 0.0), axis=-1, keepdims=True)
    gidx = gi - N_EXPERTS
    in_grp = (lane >= gidx * EXP_PER_GROUP) & (lane < (gidx + 1) * EXP_PER_GROUP)
    el = jnp.where(in_grp, logits, neg)
    emax = jnp.max(el, axis=-1, keepdims=True)
    ee = jnp.where(in_grp, jnp.exp(el - emax), 0.0)
    pe = ee / jnp.sum(ee, axis=-1, keepdims=True)
    pe_m = jnp.where(in_grp, pe, -1.0)
    v1 = jnp.max(pe_m, axis=-1, keepdims=True)
    i1 = jnp.min(jnp.where(pe_m == v1, lane, big), axis=-1, keepdims=True)
    pe_m2 = jnp.where(lane == i1, -1.0, pe_m)
    v2 = jnp.max(pe_m2, axis=-1, keepdims=True)
    i2 = jnp.min(jnp.where(pe_m2 == v2, lane, big), axis=-1, keepdims=True)
    tot = v1 + v2
    return jnp.where(lane == i1, v1 / tot * p_grp, 0.0) + jnp.where(lane == i2, v2 / tot * p_grp, 0.0)


def _mix_post_kernel(precise, batch_major, tt, bsz,
                     ya_ref, bonus_ref, g_ref, yb_ref, x_ref, g1_ref, sh2_ref, sc2_ref, wout_ref, *refs):
    wout_lo = refs[0] if precise else None
    (lnxg_ref, lnxb_ref, gones_ref, ln1g_ref, ln1b_ref, wrh_ref, wrl_ref, br_ref,
     x1_o, h2_o, comb_o, *tm_s) = refs[1:] if precise else refs

    def wout(lo_hi_ref, r0, r1):
        return None if lo_hi_ref is None else lo_hi_ref[r0:r1, :]

    def mix(t0, nt):
        nrows = nt * bsz
        steps = pl.ds(t0, nt)

        def gsum(z):
            return _group_sum(z.reshape(nrows, C_A), gones_ref[...], precise).reshape(nt, bsz, C_A)

        ya = ya_ref[steps]
        mu = gsum(ya) * (1.0 / HEAD)
        yc = ya - mu
        var = gsum(yc * yc) * (1.0 / HEAD)
        yn = yc * lax.rsqrt(var + GN_EPS) * lnxg_ref[...] + lnxb_ref[...]
        ya = (yn + bonus_ref[steps]) * g_ref[steps]
        return (_mm(_split(ya.reshape(nrows, C_A), precise), wout_ref[0:C_A, :], wout(wout_lo, 0, C_A))
                + _mm(_split(yb_ref[steps].reshape(nrows, C_B), precise), wout_ref[C_A:, :],
                      wout(wout_lo, C_A, C_A + C_B))).reshape(nt, bsz, D_MODEL)

    def post(t0, nt, y):
        nrows = nt * bsz
        steps = pl.ds(t0, nt)
        out_rows = pl.ds(t0 * bsz, nrows)
        x = _load_batch_major(x_ref, tm_s[0], t0, nt) if batch_major else x_ref[steps]
        x1 = _layer_norm(ALPHA * x + (1.0 + g1_ref[...]) * y) * ln1g_ref[...] + ln1b_ref[...]
        x1_o[steps] = x1
        h2 = (_layer_norm(x1) * (1.0 + sc2_ref[...]) + sh2_ref[...]).reshape(nrows, D_MODEL)
        hi = h2.astype(BF16)
        lo = (h2 - hi.astype(F32)).astype(BF16)
        h2_o[out_rows, :] = h2 if h2_o.dtype == F32 else hi
        logits = _dot(hi, wrh_ref[...]) + _dot(lo, wrh_ref[...]) + _dot(hi, wrl_ref[...]) + br_ref[...]
        comb_o[out_rows, :] = _route(logits)

    n_parts = MIX_PARTS if tt % MIX_PARTS == 0 else 1
    nt = tt // n_parts
    y = mix(0, nt)
    for i in range(n_parts):
        y_next = mix((i + 1) * nt, nt) if i + 1 < n_parts else None
        post(i * nt, nt, y)
        y = y_next


def _mix_post_call(ya, bonus, g, yb, x, batch_major, tt, g1, sh2, sc2, wl):
    t_len, bsz, _ = ya.shape
    rows = tt * bsz
    if batch_major:
        x_spec = pl.BlockSpec((bsz, tt, D_MODEL), lambda c: (0, c, 0))
    else:
        x_spec = pl.BlockSpec((tt, bsz, D_MODEL), lambda c: (c, 0, 0))
    seq = pl.BlockSpec((tt, bsz, C_A), lambda c: (c, 0, 0))
    precise = wl['precise']
    weights = [wl['w_out']] + ([wl['w_out_lo']] if precise else []) + [
        wl['lnx_gain'], wl['lnx_bias'], wl['gones'], wl['ln1_gain'], wl['ln1_bias'],
        wl['w_router_hi'], wl['w_router_lo'], wl['b_router']]
    scratch = [pltpu.VMEM((tt, bsz, D_MODEL), F32)] if batch_major else []
    return pl.pallas_call(
        functools.partial(_mix_post_kernel, precise, batch_major, tt, bsz),
        grid=(t_len // tt,),
        in_specs=[seq] * 4 + [x_spec] + [_full((bsz, D_MODEL))] * 3 + [_const(w.shape) for w in weights],
        out_specs=[pl.BlockSpec((tt, bsz, D_MODEL), lambda c: (c, 0, 0)),
                   pl.BlockSpec((rows, D_MODEL), lambda c: (c, 0)),
                   pl.BlockSpec((rows, ROUTER_LANES), lambda c: (c, 0))],
        out_shape=[jax.ShapeDtypeStruct((t_len, bsz, D_MODEL), F32),
                   jax.ShapeDtypeStruct((t_len * bsz, D_MODEL), F32 if wl['moe_precise'] else BF16),
                   jax.ShapeDtypeStruct((t_len * bsz, ROUTER_LANES), F32)],
        scratch_shapes=scratch,
        compiler_params=_params(),
        name="mix_post",
    )(ya, bonus, g, yb, x, g1, sh2, sc2, *weights)


def _moe_kernel(precise, batch_major_out, tt, bsz, h2_ref, comb_ref, x1_ref, g2_ref, *refs):
    wg_ref, wu_ref, wd_ref, ln2g_ref, ln2b_ref, o_ref, acc_s = refs
    weight = lambda r, i: _split(r[i], precise)
    step = pl.program_id(1)
    rows = tt * bsz

    @pl.when(step == 0)
    def _():
        acc_s[...] = jnp.zeros_like(acc_s)

    h = _split(h2_ref[...], True) if precise else (h2_ref[...], None)
    lane = lax.broadcasted_iota(jnp.int32, (rows, ROUTER_LANES), 1)
    comb = comb_ref[...]
    down = None
    for i in range(MOE_EXPERTS_PER_STEP):
        ce = jnp.sum(jnp.where(lane == step * MOE_EXPERTS_PER_STEP + i, comb, 0.0), axis=-1, keepdims=True)
        gate = _mm(h, *weight(wg_ref, i))
        hid = (gate * _sigmoid(gate)) * _mm(h, *weight(wu_ref, i))
        part = _mm(_split(hid * ce, precise), *weight(wd_ref, i))
        down = part if down is None else down + part
    acc_s[...] += down

    @pl.when(step == N_EXPERTS // MOE_EXPERTS_PER_STEP - 1)
    def _():
        moe = acc_s[...].reshape(tt, bsz, D_MODEL)
        out = _layer_norm(ALPHA * x1_ref[...] + (1.0 + g2_ref[...]) * moe) * ln2g_ref[...] + ln2b_ref[...]
        if batch_major_out:
            _store_batch_major(o_ref, out, 0, tt)
        else:
            o_ref[...] = out


def _moe_call(h2, comb, x1, batch_major_out, tt, g2, wl):
    precise = wl['moe_precise']
    t_len, bsz, _ = x1.shape
    rows = tt * bsz
    first = wl['layer'] * (N_EXPERTS // MOE_EXPERTS_PER_STEP)
    w_gate = pl.BlockSpec((MOE_EXPERTS_PER_STEP, D_MODEL, D_EXPERT), lambda c, e: (first + e, 0, 0))
    w_down = pl.BlockSpec((MOE_EXPERTS_PER_STEP, D_EXPERT, D_MODEL), lambda c, e: (first + e, 0, 0))
    weights = [wl['w_exp_gate'], wl['w_exp_up'], wl['w_exp_down']]
    w_specs = [w_gate, w_gate, w_down]
    if batch_major_out:
        o_spec = pl.BlockSpec((bsz, tt, D_MODEL), lambda c, e: (0, c, 0))
        o_shape = jax.ShapeDtypeStruct((bsz, t_len, D_MODEL), F32)
    else:
        o_spec = pl.BlockSpec((tt, bsz, D_MODEL), lambda c, e: (c, 0, 0))
        o_shape = jax.ShapeDtypeStruct((t_len, bsz, D_MODEL), F32)
    return pl.pallas_call(
        functools.partial(_moe_kernel, precise, batch_major_out, tt, bsz),
        grid=(t_len // tt, N_EXPERTS // MOE_EXPERTS_PER_STEP),
        in_specs=[pl.BlockSpec((rows, D_MODEL), lambda c, e: (c, 0)),
                  pl.BlockSpec((rows, ROUTER_LANES), lambda c, e: (c, 0)),
                  pl.BlockSpec((tt, bsz, D_MODEL), lambda c, e: (c, 0, 0)),
                  pl.BlockSpec((bsz, D_MODEL), lambda c, e: (0, 0))] + w_specs + [
                  pl.BlockSpec((1, D_MODEL), lambda c, e: (0, 0)),
                  pl.BlockSpec((1, D_MODEL), lambda c, e: (0, 0))],
        out_specs=o_spec,
        out_shape=o_shape,
        scratch_shapes=[pltpu.VMEM((rows, D_MODEL), F32)],
        compiler_params=_params(2, MOE_VMEM_LIMIT),
        name="moe",
    )(h2, comb, x1, g2, *weights, wl['ln2_gain'], wl['ln2_bias'])


def _pad_cols(w, width):
    return jnp.pad(w, ((0, 0), (0, width - w.shape[1])))


def _pad_rows(w, height):
    return jnp.pad(w, ((0, height - w.shape[0]), (0, 0)))


def _split_weight_kernel(w_ref, hi_ref, lo_ref):
    hi_ref[...], lo_ref[...] = _split(w_ref[...])


def _split_weight(w):
    w2 = w.reshape(-1, w.shape[-1])
    rows, cols = w2.shape
    tr = min(rows, 512)
    spec = pl.BlockSpec((tr, cols), lambda i: (i, 0))
    hi, lo = pl.pallas_call(
        _split_weight_kernel,
        grid=(rows // tr,),
        in_specs=[spec],
        out_specs=[spec, spec],
        out_shape=[jax.ShapeDtypeStruct(w2.shape, BF16)] * 2,
        compiler_params=_params(),
        name="split_weight",
    )(w2)
    return hi.reshape(w.shape), lo.reshape(w.shape)


def _block_diag(w):
    n, c, d = w.shape
    eye = jnp.eye(n, dtype=w.dtype)
    return (eye[:, None, :, None] * w[:, :, None, :]).reshape(n * c, n * d)


def _prep_layer(p, l, precise, moe_precise):
    row = lambda v: v[l].reshape(1, -1)
    w_in = p['w_in'][l]
    o1, o2, o3 = 3 * C_A, 3 * C_A + LORA_DECAY, 3 * C_A + LORA_DECAY + LORA_AAA
    pieces = [(0, o1, o1), (o1, o2, LORA_PAD), (o2, o3, LORA_PAD), (o3, N_SHIFT, LORA_PAD)]
    w_in_p = jnp.concatenate([_pad_cols(w_in[:, a:b], wd) for a, b, wd in pieces] + [w_in[:, N_SHIFT:]], axis=1)
    mu = p['mu_shift'][l].reshape(1, -1)
    mu_p = jnp.concatenate([_pad_cols(mu[:, a:b], wd) for a, b, wd in pieces], axis=1)
    w_router = jnp.concatenate([p['w_router_expert'][l], p['w_router_group'][l]], axis=1)
    w_router = _pad_cols(w_router, ROUTER_LANES)
    w_router_hi, w_router_lo = _split_weight(w_router)
    b_router = _pad_cols(jnp.concatenate([p['b_router_expert'][l], p['b_router_group'][l]]).reshape(1, -1),
                         ROUTER_LANES)
    head_id = jnp.arange(C_A) // HEAD
    mats = dict(
        w_in=w_in_p,
        w_decay_up=_pad_rows(p['w_decay_up'][l], LORA_PAD),
        w_a_up=_pad_rows(p['w_a_up'][l], LORA_PAD),
        w_g_up=_pad_rows(p['w_g_up'][l], LORA_PAD),
        w_gates=jnp.concatenate([_block_diag(p['w_rgate'][l]), _block_diag(p['w_igate'][l])], axis=1),
        w_out=p['w_out'][l])
    split_mats = {}
    for name, w in mats.items():
        if precise:
            split_mats[name], split_mats[name + '_lo'] = _split_weight(w)
        else:
            split_mats[name] = w.astype(BF16)
    for name in ('w_exp_gate', 'w_exp_up', 'w_exp_down'):
        split_mats[name] = p[name].reshape((-1,) + p[name].shape[2:])
    return dict(
        split_mats, layer=l, precise=precise, moe_precise=moe_precise, mu=mu_p, w0=row(p['w0']), a0=row(p['a0']),
        k_k=row(p['k_k']), k_a=row(p['k_a']), r_k=row(p['r_k']),
        conv_w=p['conv_w'][l], conv_b=row(p['conv_b']),
        b_gates=jnp.concatenate([p['b_rgate'][l], p['b_igate'][l]]).reshape(1, -1),
        lam=row(p['lru_lambda']),
        gones=(head_id[:, None] == head_id[None, :]).astype(BF16),
        lnx_gain=row(p['lnx_gain']), lnx_bias=row(p['lnx_bias']),
        ln1_gain=row(p['ln1_gain']), ln1_bias=row(p['ln1_bias']),
        w_router_hi=w_router_hi, w_router_lo=w_router_lo,
        b_router=b_router,
        ln2_gain=row(p['ln2_gain']), ln2_bias=row(p['ln2_bias']),
    )


def _trunk(x, batch_major, tt, mods, st_wkv, st_shift, st_conv, st_lru, layers, moe_tt, scan_tt):
    wkv_out, shift_out, conv_out, lru_out = [], [], [], []
    for l in range(DEPTH):
        wl = layers[l]
        sh1, sc1, g1, sh2, sc2, g2 = mods[l]
        bm_in = batch_major and l == 0
        conv_state = jnp.swapaxes(st_conv[l], 0, 1)
        (r, w, k, v, a, b, g, bonus, yb, shift_new, conv_new, lru_new) = _mix_pre_call(
            x, bm_in, tt, sh1, sc1, st_shift[l], conv_state, st_lru[l], wl)
        t_len, bsz, _ = r.shape
        if t_len > 1:
            s0 = st_wkv[l].reshape(bsz, N_HEADS, 2, I_LO, HEAD).transpose(4, 3, 2, 0, 1)
            s0 = s0.reshape(HEAD, I_LO, 2 * HEAD)
            ya, s_fin = _wkv_scan_call(a, w, b, k, r, v, s0, scan_tt)
            s_new = s_fin.reshape(HEAD, I_LO, 2, bsz, N_HEADS).transpose(3, 4, 2, 1, 0)
            s_new = s_new.reshape(bsz, N_HEADS, HEAD, HEAD)
        else:
            shp = (bsz, N_HEADS, 1, HEAD)
            y4, s_new = _wkv_step_call(a.reshape(shp), w.reshape(shp), b.reshape(shp), k.reshape(shp),
                                       r.reshape(shp), v.reshape(shp), st_wkv[l], WKV_STEP_BATCH)
            ya = y4.reshape(1, bsz, C_A)
        post_tt = POST_TT if t_len % POST_TT == 0 else tt
        x1, h2, comb = _mix_post_call(ya, bonus, g, yb, x, bm_in, post_tt, g1, sh2, sc2, wl)
        bm_out = batch_major and l == DEPTH - 1
        x = _moe_call(h2, comb, x1, bm_out, moe_tt, g2, wl)
        wkv_out.append(s_new)
        shift_out.append(shift_new)
        conv_out.append(jnp.swapaxes(conv_new, 0, 1))
        lru_out.append(lru_new)
    return x, jnp.stack(wkv_out), jnp.stack(shift_out), jnp.stack(conv_out), jnp.stack(lru_out)


def kernel(x_prompt, x_sample, c_prompt, c_sample, state_wkv, state_shift, state_conv, state_lru, w_ada, b_ada, w_in, mu_shift, w0, w_decay_up, a0, w_a_up, w_g_up, k_k, k_a, r_k, lnx_gain, lnx_bias, conv_w, conv_b, w_rgate, b_rgate, w_igate, b_igate, lru_lambda, w_out, ln1_gain, ln1_bias, w_router_group, b_router_group, w_router_expert, b_router_expert, w_exp_gate, w_exp_up, w_exp_down, ln2_gain, ln2_bias):
    p = dict(w_in=w_in, mu_shift=mu_shift, w0=w0, w_decay_up=w_decay_up, a0=a0, w_a_up=w_a_up,
             w_g_up=w_g_up, k_k=k_k, k_a=k_a, r_k=r_k.reshape(DEPTH, C_A), lnx_gain=lnx_gain,
             lnx_bias=lnx_bias, conv_w=conv_w, conv_b=conv_b, w_rgate=w_rgate, b_rgate=b_rgate,
             w_igate=w_igate, b_igate=b_igate, lru_lambda=lru_lambda, w_out=w_out, ln1_gain=ln1_gain,
             ln1_bias=ln1_bias, w_router_group=w_router_group, b_router_group=b_router_group,
             w_router_expert=w_router_expert, b_router_expert=b_router_expert, w_exp_gate=w_exp_gate,
             w_exp_up=w_exp_up, w_exp_down=w_exp_down, ln2_gain=ln2_gain, ln2_bias=ln2_bias)
    prepared = [_prep_layer(p, l, precise=True, moe_precise=(l < DEPTH - 1)) for l in range(DEPTH)]
    layers_s = prepared
    layers_p = [dict(prepared[l], precise=(l == 0), moe_precise=False) for l in range(DEPTH)]
    bp, bs = x_prompt.shape[0], x_sample.shape[0]

    mod = _ada_call(jnp.concatenate([c_sample, c_prompt], axis=0), w_ada, b_ada)
    split = lambda m: [m[:, i * D_MODEL:(i + 1) * D_MODEL] for i in range(6)]
    mods_s = [split(mod[l, :bs]) for l in range(DEPTH)]
    mods_p = [split(mod[l, bs:bs + bp]) for l in range(DEPTH)]

    z_wkv = jnp.zeros((DEPTH, bp, N_HEADS, HEAD, HEAD), F32)
    z_shift = jnp.zeros((DEPTH, bp, D_MODEL), F32)
    z_conv = jnp.zeros((DEPTH, bp, CONV_W - 1, C_B), F32)
    z_lru = jnp.zeros((DEPTH, bp, C_B), F32)
    y_p, wkv_p, shift_p, conv_p, lru_p = _trunk(x_prompt, True, MIX_TT, mods_p, z_wkv, z_shift, z_conv, z_lru,
                                                layers_p, MOE_TT, SCAN_TT)
    xs_tm = x_sample.reshape(1, bs, D_MODEL)
    y_s, wkv_s, shift_s, conv_s, lru_s = _trunk(xs_tm, False, 1, mods_s, state_wkv, state_shift, state_conv,
                                                state_lru, layers_s, 1, 1)
    return (y_p, y_s.reshape(bs, 1, D_MODEL), wkv_p, shift_p, conv_p, lru_p, wkv_s, shift_s, conv_s, lru_s)
```
